```python
import math
import jax, jax.numpy as jnp
from jax import lax
import numpy as np

D_MODEL = 1024
BATCH = 4
SEQ = 8192
DEPTH = 1

CHUNK = 64
GM_WIDTH = 512
GM_GROUPS = 8
GM_BLOCK = 128
FOX_HEADS = 8
FOX_HEAD_DIM = 64
FOX_WIDTH = FOX_HEADS * FOX_HEAD_DIM
FOX_Q_BLOCK = 128
MEM_LEN = 256
MEM_HEADS = 4
MEM_HEAD_DIM = D_MODEL // MEM_HEADS
N_EXPERTS = 32
TOP_K = 4
D_EXPERT = D_MODEL
SWIGLU_ALPHA = 1.702
SWIGLU_LIMIT = 7.0
MOE_BLOCK = 512
LN_EPS = 1e-5
DEEPNORM_ALPHA = (2 * DEPTH) ** 0.25
DEEPNORM_BETA = (8 * DEPTH) ** -0.25
IN_SIZES = (GM_WIDTH, GM_WIDTH, FOX_WIDTH, FOX_WIDTH, FOX_WIDTH, FOX_HEADS, D_MODEL, D_MODEL)
IN_COLS = sum(IN_SIZES)

kernel_name = "hybrid_gmlp_fox_memxattn_moe_deepnorm"


def layer_norm(x, g, b):
    xf = x.astype(jnp.float32)
    mu = jnp.mean(xf, axis=-1, keepdims=True)
    var = jnp.mean(jnp.square(xf - mu), axis=-1, keepdims=True)
    return ((xf - mu) * lax.rsqrt(var + LN_EPS) * g + b).astype(x.dtype)


def spatial_gating(u, v, ln_v_g, ln_v_b, w_sp, b_sp):
    B_, S_, _ = v.shape
    v = layer_norm(v, ln_v_g, ln_v_b)
    pos = jnp.arange(GM_BLOCK)
    chunk_mask = (pos[None, :] // CHUNK) <= (pos[:, None] // CHUNK)
    w = jnp.where(chunk_mask[None], w_sp, 0)
    vb = v.reshape(B_, S_ // GM_BLOCK, GM_BLOCK, GM_GROUPS, GM_WIDTH // GM_GROUPS)
    mixed = jnp.einsum('gts,bnsgc->bntgc', w, vb) + b_sp.T[None, None, :, :, None]
    return u * mixed.reshape(B_, S_, GM_WIDTH)


def forgetting_attention(q, k, v, log_f):
    S_ = q.shape[2]
    c = jnp.cumsum(log_f, axis=-1)
    scale = FOX_HEAD_DIM ** -0.5
    outs = []
    for i in range(S_ // FOX_Q_BLOCK):
        q0, q1 = i * FOX_Q_BLOCK, (i + 1) * FOX_Q_BLOCK
        logits = jnp.einsum('bhqd,bhkd->bhqk', q[:, :, q0:q1], k[:, :, :q1]).astype(jnp.float32) * scale
        logits = logits + c[:, :, q0:q1, None] - c[:, :, None, :q1]
        causal = jnp.arange(q0, q1)[:, None] >= jnp.arange(q1)[None, :]
        p = jax.nn.softmax(jnp.where(causal, logits, -jnp.inf), axis=-1)
        outs.append(jnp.einsum('bhqk,bhkd->bhqd', p.astype(v.dtype), v[:, :, :q1]))
    return jnp.concatenate(outs, axis=2)


def memory_cross_attention(x, mem, w_mq, w_mkv, w_mo):
    B_, S_, _ = x.shape
    M_ = mem.shape[1]
    q = (x @ w_mq).reshape(B_, S_, MEM_HEADS, MEM_HEAD_DIM)
    kv = (mem @ w_mkv).reshape(B_, M_, 2, MEM_HEADS, MEM_HEAD_DIM)
    k, v = kv[:, :, 0], kv[:, :, 1]
    logits = jnp.einsum('bshd,bmhd->bhsm', q, k).astype(jnp.float32) * (MEM_HEAD_DIM ** -0.5)
    p = jax.nn.softmax(logits, axis=-1)
    o = jnp.einsum('bhsm,bmhd->bshd', p.astype(v.dtype), v).reshape(B_, S_, D_MODEL)
    return o @ w_mo


def moe_ffn(x, w_router, b_router, w_gate_up, b_gate_up, w_down, b_down):
    B_, S_, D_ = x.shape
    xt = x.reshape(-1, D_)
    T = xt.shape[0]
    logits = (xt @ w_router + b_router).astype(jnp.float32)
    top_val, top_idx = lax.top_k(logits, TOP_K)
    gate_w = jax.nn.softmax(top_val, axis=-1)
    A = T * TOP_K
    e_flat = top_idx.reshape(-1)
    tok_flat = jnp.arange(A, dtype=jnp.int32) // TOP_K
    w_flat = gate_w.reshape(-1)
    order = jnp.argsort(e_flat)
    e_sorted = e_flat[order]
    counts = jnp.bincount(e_flat, length=N_EXPERTS)
    padded = (counts + MOE_BLOCK - 1) // MOE_BLOCK * MOE_BLOCK
    start_sorted = jnp.cumsum(counts) - counts
    end_padded = jnp.cumsum(padded)
    start_padded = end_padded - padded
    dest = start_padded[e_sorted] + jnp.arange(A, dtype=jnp.int32) - start_sorted[e_sorted]
    n_blocks = -(-(A + N_EXPERTS * (MOE_BLOCK - 1)) // MOE_BLOCK)
    P = n_blocks * MOE_BLOCK
    tok_pad = jnp.zeros((P,), jnp.int32).at[dest].set(tok_flat[order])
    w_pad = jnp.zeros((P,), jnp.float32).at[dest].set(w_flat[order])
    block_start = jnp.arange(n_blocks, dtype=jnp.int32) * MOE_BLOCK
    block_e = jnp.minimum(jnp.searchsorted(end_padded, block_start, side='right'), N_EXPERTS - 1)

    def expert_block(args):
        tok, e = args
        h = xt[tok] @ w_gate_up[e] + b_gate_up[e]
        gate, up = h[:, :D_EXPERT], h[:, D_EXPERT:]
        gate = jnp.minimum(gate, SWIGLU_LIMIT)
        up = jnp.clip(up, -SWIGLU_LIMIT, SWIGLU_LIMIT)
        glu = gate * jax.nn.sigmoid(SWIGLU_ALPHA * gate)
        return ((up + 1) * glu) @ w_down[e] + b_down[e]

    y = lax.map(expert_block, (tok_pad.reshape(n_blocks, MOE_BLOCK), block_e))
    y = y.reshape(P, D_) * w_pad[:, None].astype(y.dtype)
    out = jax.ops.segment_sum(y, tok_pad, num_segments=T)
    return out.reshape(B_, S_, D_)


def setup_inputs(seed: int = 0) -> dict:
    key = jax.random.key(seed)
    ks = jax.random.split(key, 32)
    L = DEPTH

    def nrm(k, shape, scale):
        return jax.random.normal(k, shape, jnp.float32) * scale

    def gain(k, shape):
        return 1.0 + 0.05 * jax.random.normal(k, shape, jnp.float32)

    return {
        "x": nrm(ks[0], (BATCH, SEQ, D_MODEL), 1.0),
        "mem": nrm(ks[1], (BATCH, MEM_LEN, D_MODEL), 1.0),
        "ln_in_g": gain(ks[2], (D_MODEL,)),
        "ln_in_b": nrm(ks[3], (D_MODEL,), 0.02),
        "w_in": nrm(ks[4], (L, D_MODEL, IN_COLS), D_MODEL ** -0.5),
        "b_forget": 2.0 + nrm(ks[5], (L, FOX_HEADS), 0.5),
        "ln_v_g": gain(ks[6], (L, GM_WIDTH)),
        "ln_v_b": nrm(ks[7], (L, GM_WIDTH), 0.02),
        "w_spatial": nrm(ks[8], (L, GM_GROUPS, GM_BLOCK, GM_BLOCK), GM_BLOCK ** -0.5),
        "b_spatial": 1.0 + nrm(ks[9], (L, GM_GROUPS, GM_BLOCK), 0.1),
        "w_branch_a": nrm(ks[10], (L, GM_WIDTH, D_MODEL), GM_WIDTH ** -0.5),
        "w_branch_b": nrm(ks[11], (L, FOX_WIDTH, D_MODEL), FOX_WIDTH ** -0.5),
        "w_out": nrm(ks[12], (L, D_MODEL, D_MODEL), DEEPNORM_BETA * D_MODEL ** -0.5),
        "ln1_g": gain(ks[13], (L, D_MODEL)),
        "ln1_b": nrm(ks[14], (L, D_MODEL), 0.02),
        "ln_mem_g": gain(ks[15], (L, D_MODEL)),
        "ln_mem_b": nrm(ks[16], (L, D_MODEL), 0.02),
        "w_mq": nrm(ks[17], (L, D_MODEL, D_MODEL), D_MODEL ** -0.5),
        "w_mkv": nrm(ks[18], (L, D_MODEL, 2 * D_MODEL), D_MODEL ** -0.5),
        "w_mo": nrm(ks[19], (L, D_MODEL, D_MODEL), DEEPNORM_BETA * D_MODEL ** -0.5),
        "ln2_g": gain(ks[20], (L, D_MODEL)),
        "ln2_b": nrm(ks[21], (L, D_MODEL), 0.02),
        "w_router": nrm(ks[22], (L, D_MODEL, N_EXPERTS), D_MODEL ** -0.5),
        "b_router": nrm(ks[23], (L, N_EXPERTS), 0.01),
        "w_gate_up": nrm(ks[24], (L, N_EXPERTS, D_MODEL, 2 * D_EXPERT), D_MODEL ** -0.5),
        "b_gate_up": nrm(ks[25], (L, N_EXPERTS, 2 * D_EXPERT), 0.01),
        "w_down": nrm(ks[26], (L, N_EXPERTS, D_EXPERT, D_MODEL), DEEPNORM_BETA * D_EXPERT ** -0.5),
        "b_down": nrm(ks[27], (L, N_EXPERTS, D_MODEL), 0.01),
        "ln3_g": gain(ks[28], (L, D_MODEL)),
        "ln3_b": nrm(ks[29], (L, D_MODEL), 0.02),
    }


def reference(x, mem, ln_in_g, ln_in_b, w_in, b_forget, ln_v_g, ln_v_b, w_spatial, b_spatial,
              w_branch_a, w_branch_b, w_out, ln1_g, ln1_b, ln_mem_g, ln_mem_b, w_mq, w_mkv, w_mo,
              ln2_g, ln2_b, w_router, b_router, w_gate_up, b_gate_up, w_down, b_down, ln3_g, ln3_b):
    B_, S_, _ = x.shape
    split_at = [int(i) for i in np.cumsum(IN_SIZES)[:-1]]
    x = layer_norm(x, ln_in_g, ln_in_b)
    for l in range(DEPTH):
        h = x @ w_in[l]
        u_gm, v_gm, q, k, v, f_logit, g_a, g_b = jnp.split(h, split_at, axis=-1)
        y_a = spatial_gating(jax.nn.gelu(u_gm, approximate=False), jax.nn.gelu(v_gm, approximate=False),
                             ln_v_g[l], ln_v_b[l], w_spatial[l], b_spatial[l])
        to_heads = lambda t: t.reshape(B_, S_, FOX_HEADS, FOX_HEAD_DIM).transpose(0, 2, 1, 3)
        log_f = jax.nn.log_sigmoid((f_logit + b_forget[l]).astype(jnp.float32)).transpose(0, 2, 1)
        y_b = forgetting_attention(to_heads(q), to_heads(k), to_heads(v), log_f)
        y_b = y_b.transpose(0, 2, 1, 3).reshape(B_, S_, FOX_WIDTH)
        merged = jax.nn.sigmoid(g_a) * (y_a @ w_branch_a[l]) + jax.nn.sigmoid(g_b) * (y_b @ w_branch_b[l])
        x = layer_norm(DEEPNORM_ALPHA * x + merged @ w_out[l], ln1_g[l], ln1_b[l])
        mem_n = layer_norm(mem, ln_mem_g[l], ln_mem_b[l])
        x = layer_norm(DEEPNORM_ALPHA * x + memory_cross_attention(x, mem_n, w_mq[l], w_mkv[l], w_mo[l]),
                       ln2_g[l], ln2_b[l])
        y_moe = moe_ffn(x, w_router[l], b_router[l], w_gate_up[l], b_gate_up[l], w_down[l], b_down[l])
        x = layer_norm(DEEPNORM_ALPHA * x + y_moe, ln3_g[l], ln3_b[l])
    return x
```

```python
import functools
import math

import jax
import jax.numpy as jnp
from jax import lax
from jax.experimental import pallas as pl
from jax.experimental.pallas import tpu as pltpu

D_MODEL = 1024
CHUNK = 64
GM_WIDTH = 512
GM_GROUPS = 8
GM_BLOCK = 128
FOX_HEADS = 8
FOX_HEAD_DIM = 64
FOX_WIDTH = FOX_HEADS * FOX_HEAD_DIM
MEM_HEADS = 4
MEM_HEAD_DIM = D_MODEL // MEM_HEADS
N_EXPERTS = 32
TOP_K = 4
D_EXPERT = D_MODEL
SWIGLU_ALPHA = 1.702
SWIGLU_LIMIT = 7.0
LN_EPS = 1e-5
DEPTH = 1
DEEPNORM_ALPHA = (2 * DEPTH) ** 0.25

LANES = 128
AUG_LANES = 2 * FOX_HEAD_DIM
N_SPLIT = 3
VMEM_LIMIT = 56 * 1024 * 1024
NEG_BIG = -1e30

TM_IN = 256
TQ = 512
TK = 512
TM_POST = 256
RT = 512
TM_EXP = 512

bf16 = jnp.bfloat16
f32 = jnp.float32


def _ln(x, g, b):
    mu = jnp.mean(x, axis=-1, keepdims=True)
    xc = x - mu
    var = jnp.mean(xc * xc, axis=-1, keepdims=True)
    return xc * lax.rsqrt(var + LN_EPS) * g + b


def _dot(a, b):
    return jnp.dot(a, b, preferred_element_type=f32)


def _dot_nt(a, b):
    return lax.dot_general(a, b, (((1,), (1,)), ((), ())), preferred_element_type=f32)


def _params(sem):
    return pltpu.CompilerParams(dimension_semantics=sem, vmem_limit_bytes=VMEM_LIMIT)


def _const_spec(shape):
    nd = len(shape)
    return pl.BlockSpec(shape, lambda *_: (0,) * nd)


def _mem_kv_kernel(mem_ref, g_ref, b_ref, w_ref, k_ref, v_ref):
    mn = _ln(mem_ref[...], g_ref[...], b_ref[...]).astype(bf16)
    kv = _dot(mn, w_ref[...])
    k_ref[...] = kv[:, :D_MODEL].astype(bf16)
    v_ref[...] = kv[:, D_MODEL:].astype(bf16)


def _mem_kv(mem2, g, b, w_mkv, m_len):
    rows = mem2.shape[0]
    return pl.pallas_call(
        _mem_kv_kernel,
        out_shape=(jax.ShapeDtypeStruct((rows, D_MODEL), bf16),) * 2,
        grid=(rows // m_len,),
        in_specs=[pl.BlockSpec((m_len, D_MODEL), lambda i: (i, 0)),
                  _const_spec((1, D_MODEL)), _const_spec((1, D_MODEL)),
                  _const_spec((D_MODEL, 2 * D_MODEL))],
        out_specs=(pl.BlockSpec((m_len, D_MODEL), lambda i: (i, 0)),) * 2,
        compiler_params=_params(("arbitrary",)),
        name="mem_kv",
    )(mem2, g, b, w_mkv)


def _in_proj_kernel(tiles_per_seq, x_ref, g_ref, b_ref, wuv_ref, wq_ref, wk_ref, wv_ref, wg_ref,
                    bf_ref, lvg_ref, lvb_ref,
                    xn_ref, gu_ref, vn_ref, q_ref, k_ref, v_ref, sa_ref, sb_ref, carry_ref):
    tm = x_ref.shape[0]
    xn = _ln(x_ref[...], g_ref[...], b_ref[...])
    xn_ref[...] = xn
    xb = xn.astype(bf16)

    huv = _dot(xb, wuv_ref[...])
    gel = 0.5 * huv * (1.0 + lax.erf(huv * (1.0 / math.sqrt(2.0))))
    gu_ref[...] = gel[:, :GM_WIDTH].astype(bf16)
    vn_ref[...] = _ln(gel[:, GM_WIDTH:], lvg_ref[...], lvb_ref[...]).astype(bf16)

    v_ref[...] = _dot(xb, wv_ref[...]).astype(bf16)

    hg = _dot(xb, wg_ref[...])
    sg = jax.nn.sigmoid(hg)
    sa_ref[...] = sg[:, :D_MODEL].astype(bf16)
    sb_ref[...] = sg[:, D_MODEL:].astype(bf16)

    hq = _dot(xb, wq_ref[...])
    hk = _dot(xb, wk_ref[...])
    lane = lax.broadcasted_iota(jnp.int32, hq.shape, 1) % AUG_LANES
    row = lax.broadcasted_iota(jnp.int32, hq.shape, 0)
    sub = lane - FOX_HEAD_DIM
    in_c = (sub >= 0) & (sub < N_SPLIT)
    in_one = (sub >= N_SPLIT) & (sub < 2 * N_SPLIT)
    z = hq + bf_ref[...]
    log_f = jnp.minimum(z, 0.0) - jnp.log1p(jnp.exp(-jnp.abs(z)))
    c = jnp.where(in_c | in_one, log_f, 0.0)
    shift = 1
    while shift < tm:
        c = c + jnp.where(row >= shift, pltpu.roll(c, shift, 0), 0.0)
        shift *= 2

    @pl.when(pl.program_id(0) % tiles_per_seq == 0)
    def _():
        carry_ref[...] = jnp.zeros_like(carry_ref)

    c = c + carry_ref[...]
    carry_ref[...] = c[tm - 1:tm, :]
    hi = c.astype(bf16).astype(f32)
    r1 = c - hi
    mid = r1.astype(bf16).astype(f32)
    lo = r1 - mid
    piece = jnp.where(sub % N_SPLIT == 0, hi, jnp.where(sub % N_SPLIT == 1, mid, lo))
    is_qk = sub < 0
    q_ref[...] = jnp.where(is_qk, hq, jnp.where(in_c, piece, jnp.where(in_one, 1.0, 0.0))).astype(bf16)
    k_ref[...] = jnp.where(is_qk, hk, jnp.where(in_c, 1.0, jnp.where(in_one, -piece, 0.0))).astype(bf16)


def _in_proj(x2, g, b, wuv, wq, wk, wv, wg, bf_aug, lvg, lvb, seq):
    t = x2.shape[0]
    tm = TM_IN
    row = lambda w: pl.BlockSpec((tm, w), lambda i: (i, 0))
    outs = [(D_MODEL, f32), (GM_WIDTH, bf16), (GM_WIDTH, bf16), (FOX_HEADS * AUG_LANES, bf16),
            (FOX_HEADS * AUG_LANES, bf16), (FOX_WIDTH, bf16), (D_MODEL, bf16), (D_MODEL, bf16)]
    return pl.pallas_call(
        functools.partial(_in_proj_kernel, seq // tm),
        out_shape=tuple(jax.ShapeDtypeStruct((t, w), dt) for w, dt in outs),
        grid=(t // tm,),
        in_specs=[row(D_MODEL), _const_spec(g.shape), _const_spec(b.shape),
                  _const_spec(wuv.shape), _const_spec(wq.shape), _const_spec(wk.shape),
                  _const_spec(wv.shape), _const_spec(wg.shape), _const_spec(bf_aug.shape),
                  _const_spec(lvg.shape), _const_spec(lvb.shape)],
        out_specs=tuple(row(w) for w, _ in outs),
        scratch_shapes=[pltpu.VMEM((1, FOX_HEADS * AUG_LANES), f32)],
        compiler_params=_params(("arbitrary",)),
        name="in_proj",
    )(x2, g, b, wuv, wq, wk, wv, wg, bf_aug, lvg, lvb)


def _fox_attn_kernel(q_ref, k_ref, v_ref, o_ref):
    i = pl.program_id(2)
    tq = q_ref.shape[1]
    lane = lax.broadcasted_iota(jnp.int32, (tq, LANES), 1)
    r_ids = lax.broadcasted_iota(jnp.int32, (tq, TK), 0)
    c_ids = lax.broadcasted_iota(jnp.int32, (tq, TK), 1)
    heads_out = []
    for hh in range(2):
        q = q_ref[0, :, hh * AUG_LANES:(hh + 1) * AUG_LANES]

        def step(j, carry, masked):
            m, l, acc = carry
            start = pl.multiple_of(j * TK, TK)
            kj = k_ref[0, pl.ds(start, TK), hh * AUG_LANES:(hh + 1) * AUG_LANES]
            vj = v_ref[0, pl.ds(start, TK), :]
            s = _dot_nt(q, kj)
            if masked:
                s = jnp.where(r_ids >= c_ids, s, NEG_BIG)
            m_new = jnp.maximum(m, jnp.max(s, axis=-1, keepdims=True))
            alpha = jnp.exp(m - m_new)
            p = jnp.exp(s - m_new)
            l = alpha * l + jnp.sum(p, axis=-1, keepdims=True)
            acc = alpha * acc + _dot(p.astype(bf16), vj)
            return m_new, l, acc

        init = (jnp.full((tq, 1), NEG_BIG, f32), jnp.zeros((tq, 1), f32), jnp.zeros((tq, LANES), f32))
        carry = lax.fori_loop(0, i, functools.partial(step, masked=False), init)
        m, l, acc = step(i, carry, True)
        heads_out.append(acc / l)
    o_ref[0] = jnp.where(lane < FOX_HEAD_DIM, heads_out[0], heads_out[1]).astype(bf16)


def _fox_attn(q_aug, k_aug, v):
    b, s, _ = v.shape
    assert TQ == TK
    return pl.pallas_call(
        _fox_attn_kernel,
        out_shape=jax.ShapeDtypeStruct((b, s, FOX_WIDTH), bf16),
        grid=(b, FOX_HEADS // 2, s // TQ),
        in_specs=[pl.BlockSpec((1, TQ, 2 * AUG_LANES), lambda bi, hp, i: (bi, i, hp)),
                  pl.BlockSpec((1, s, 2 * AUG_LANES), lambda bi, hp, i: (bi, 0, hp)),
                  pl.BlockSpec((1, s, LANES), lambda bi, hp, i: (bi, 0, hp))],
        out_specs=pl.BlockSpec((1, TQ, LANES), lambda bi, hp, i: (bi, i, hp)),
        compiler_params=_params(("arbitrary", "arbitrary", "arbitrary")),
        name="fox_attn",
    )(q_aug, k_aug, v)


def _post_mix_kernel(gu_ref, vn_ref, sa_ref, sb_ref, yb_ref, xn_ref, km_ref, vm_ref,
                     wsp_ref, bsp_ref, wa_ref, wb_ref, wo_ref, g1_ref, b1_ref,
                     wmq_ref, wmo_ref, g2_ref, b2_ref, wr_ref, br_ref,
                     x2_ref, x2b_ref, ri_ref, rw_ref, cnt_ref):
    tm = gu_ref.shape[0]
    lane = lax.broadcasted_iota(jnp.int32, (GM_BLOCK, LANES), 1)
    blocks = []
    for r in range(tm // GM_BLOCK):
        pairs = []
        for p in range(GM_GROUPS // 2):
            vp = vn_ref[r * GM_BLOCK:(r + 1) * GM_BLOCK, p * LANES:(p + 1) * LANES]
            lo = _dot(wsp_ref[2 * p], vp)
            hi = _dot(wsp_ref[2 * p + 1], vp)
            pairs.append(jnp.where(lane < GM_WIDTH // GM_GROUPS, lo, hi))
        blocks.append(jnp.concatenate(pairs, axis=1) + bsp_ref[...])
    mixed = jnp.concatenate(blocks, axis=0)
    y_a = (gu_ref[...].astype(f32) * mixed).astype(bf16)
    merged = (sa_ref[...].astype(f32) * _dot(y_a, wa_ref[...])
              + sb_ref[...].astype(f32) * _dot(yb_ref[...], wb_ref[...]))
    x1 = _ln(DEEPNORM_ALPHA * xn_ref[...] + _dot(merged.astype(bf16), wo_ref[...]), g1_ref[...], b1_ref[...])
    qm = (_dot(x1.astype(bf16), wmq_ref[...]) * (MEM_HEAD_DIM ** -0.5)).astype(bf16)
    heads = []
    for h in range(MEM_HEADS):
        sl = slice(h * MEM_HEAD_DIM, (h + 1) * MEM_HEAD_DIM)
        s = _dot_nt(qm[:, sl], km_ref[:, sl])
        s = s - jnp.max(s, axis=-1, keepdims=True)
        p = jnp.exp(s)
        p = p / jnp.sum(p, axis=-1, keepdims=True)
        heads.append(_dot(p.astype(bf16), vm_ref[:, sl]))
    o = jnp.concatenate(heads, axis=1).astype(bf16)
    x2 = _ln(DEEPNORM_ALPHA * x1 + _dot(o, wmo_ref[...]), g2_ref[...], b2_ref[...])
    x2_ref[...] = x2
    x2b_ref[...] = x2.astype(bf16)
    logits = jnp.dot(x2, wr_ref[...], precision=lax.Precision.HIGHEST, preferred_element_type=f32) + br_ref[...]
    lane_t = lax.broadcasted_iota(jnp.int32, (tm, LANES), 1)
    work = jnp.where(lane_t < N_EXPERTS, logits, -jnp.inf)
    tops, idxs = [], []
    for _ in range(TOP_K):
        mk = jnp.max(work, axis=-1, keepdims=True)
        ik = jnp.min(jnp.where(work == mk, lane_t, LANES), axis=-1, keepdims=True)
        work = jnp.where(lane_t == ik, -jnp.inf, work)
        tops.append(mk)
        idxs.append(ik)
    exps = [jnp.exp(tk - tops[0]) for tk in tops]
    denom = exps[0] + exps[1] + exps[2] + exps[3]
    member = jnp.zeros((tm, LANES), f32)
    ri = jnp.zeros((tm, LANES), jnp.int32)
    rw = jnp.zeros((tm, LANES), f32)
    for k in range(TOP_K):
        member = member + jnp.where(lane_t == idxs[k], 1.0, 0.0)
        ri = jnp.where(lane_t == k, idxs[k], ri)
        rw = jnp.where(lane_t == k, exps[k] / denom, rw)
    ri_ref[...] = ri
    rw_ref[...] = rw
    cnt_ref[0] = jnp.sum(member, axis=0, keepdims=True)


def _post_mix(gu, vn, sa, sb, yb, xn, k_mem, v_mem, wsp, bsp, wa, wb, wo, g1, b1, wmq, wmo, g2, b2, wr, br,
              seq, m_len):
    t = gu.shape[0]
    tm = TM_POST
    per_seq = seq // tm
    row = lambda w: pl.BlockSpec((tm, w), lambda i: (i, 0))
    memspec = pl.BlockSpec((m_len, D_MODEL), lambda i: (i // per_seq, 0))
    consts = [wsp, bsp, wa, wb, wo, g1, b1, wmq, wmo, g2, b2, wr, br]
    return pl.pallas_call(
        _post_mix_kernel,
        out_shape=(jax.ShapeDtypeStruct((t, D_MODEL), f32),
                   jax.ShapeDtypeStruct((t, D_MODEL), bf16),
                   jax.ShapeDtypeStruct((t, LANES), jnp.int32),
                   jax.ShapeDtypeStruct((t, LANES), f32),
                   jax.ShapeDtypeStruct((t // tm, 1, LANES), f32)),
        grid=(t // tm,),
        in_specs=[row(GM_WIDTH), row(GM_WIDTH), row(D_MODEL), row(D_MODEL), row(FOX_WIDTH), row(D_MODEL),
                  memspec, memspec] + [_const_spec(c.shape) for c in consts],
        out_specs=(row(D_MODEL), row(D_MODEL), row(LANES), row(LANES),
                   pl.BlockSpec((1, 1, LANES), lambda i: (i, 0, 0))),
        compiler_params=_params(("arbitrary",)),
        name="post_mix",
    )(gu, vn, sa, sb, yb, xn, k_mem, v_mem, *consts)


def _pack_halves(a, b):
    return (lax.bitcast_convert_type(a, jnp.uint32)
            | (lax.bitcast_convert_type(b, jnp.uint32) >> jnp.uint32(16)))


def _unpack_halves(p):
    a = lax.bitcast_convert_type(p & jnp.uint32(0xFFFF0000), f32)
    b = lax.bitcast_convert_type(p << jnp.uint32(16), f32)
    return a.astype(bf16), b.astype(bf16)


ROW_WORDS = D_MODEL // 2
ROW_SHAPE = (ROW_WORDS // LANES, LANES)


def _load_rows(ref):
    return jnp.concatenate([ref[:, j, :] for j in range(ROW_SHAPE[0])], axis=1)


def _store_rows(ref, val):
    for j in range(ROW_SHAPE[0]):
        ref[:, j, :] = val[:, j * LANES:(j + 1) * LANES]


def _segment_copies(cnt_s, lst_s, dst_s, tile, local_ref, global_ref, sem, to_global, op):
    def per_expert(e, _):
        n = cnt_s[tile * N_EXPERTS + e]
        ls = lst_s[tile * N_EXPERTS + e]
        gd = dst_s[tile * N_EXPERTS + e]
        size = RT
        while size >= 1:
            done = (n // (2 * size)) * (2 * size)

            @pl.when((n & size) != 0)
            def _(size=size, done=done):
                loc = local_ref.at[pl.ds(ls + done, size)]
                glo = global_ref.at[pl.ds(gd + done, size)]
                cp = pltpu.make_async_copy(loc, glo, sem) if to_global else pltpu.make_async_copy(glo, loc, sem)
                cp.start() if op == "start" else cp.wait()

            size //= 2
        return 0

    lax.fori_loop(0, N_EXPERTS, per_expert, 0)


def _dispatch_kernel(cnt_s, lst_s, dst_s, ri_ref, lstc_ref, x_ref, xs_ref, xg_ref, sem):
    i = pl.program_id(0)
    slot = i % 2
    rt = x_ref.shape[0]
    idx_t = jnp.transpose(ri_ref[...].astype(f32))
    sub = lax.broadcasted_iota(jnp.int32, (LANES, rt), 0).astype(f32)
    member_t = jnp.zeros((LANES, rt), f32)
    for k in range(TOP_K):
        member_t = member_t + jnp.where(sub == idx_t[k:k + 1, :], 1.0, 0.0)
    r_ids = lax.broadcasted_iota(jnp.int32, (rt, rt), 0)
    c_ids = lax.broadcasted_iota(jnp.int32, (rt, rt), 1)
    earlier = jnp.where(r_ids < c_ids, 1.0, 0.0).astype(bf16)
    base = lstc_ref[0] + _dot(member_t.astype(bf16), earlier)
    row_i = lax.broadcasted_iota(jnp.int32, (TOP_K * rt, rt), 0)
    sel = jnp.zeros((TOP_K * rt, rt), f32)
    for k in range(TOP_K):
        lpos = jnp.sum(jnp.where(sub == idx_t[k:k + 1, :], base, 0.0), axis=0, keepdims=True)
        sel = jnp.where(row_i == lpos.astype(jnp.int32), 1.0, sel)
    sel = sel.astype(bf16)
    half = D_MODEL // 2
    packed = _pack_halves(_dot(sel, x_ref[:, :half]), _dot(sel, x_ref[:, half:]))

    @pl.when(i > 0)
    def _():
        _segment_copies(cnt_s, lst_s, dst_s, i - 1, xg_ref.at[1 - slot], xs_ref, sem.at[1 - slot], True, "wait")

    _store_rows(xg_ref.at[slot], packed)
    _segment_copies(cnt_s, lst_s, dst_s, i, xg_ref.at[slot], xs_ref, sem.at[slot], True, "start")

    @pl.when(i == pl.num_programs(0) - 1)
    def _():
        _segment_copies(cnt_s, lst_s, dst_s, i, xg_ref.at[slot], xs_ref, sem.at[slot], True, "wait")


def _dispatch(cnt_s, lst_s, dst_s, ri, lst_col, x2b):
    t = x2b.shape[0]
    grid_spec = pltpu.PrefetchScalarGridSpec(
        num_scalar_prefetch=3,
        grid=(t // RT,),
        in_specs=[pl.BlockSpec((RT, LANES), lambda i, *_: (i, 0)),
                  pl.BlockSpec((1, LANES, 1), lambda i, *_: (i, 0, 0)),
                  pl.BlockSpec((RT, D_MODEL), lambda i, *_: (i, 0))],
        out_specs=pl.BlockSpec(memory_space=pl.ANY),
        scratch_shapes=[pltpu.VMEM((2, TOP_K * RT) + ROW_SHAPE, jnp.uint32), pltpu.SemaphoreType.DMA((2,))],
    )
    return pl.pallas_call(
        _dispatch_kernel,
        out_shape=jax.ShapeDtypeStruct((t * TOP_K,) + ROW_SHAPE, jnp.uint32),
        grid_spec=grid_spec,
        compiler_params=_params(("arbitrary",)),
        name="dispatch",
    )(cnt_s, lst_s, dst_s, ri, lst_col, x2b)


def _experts_kernel(tw_s, ew_s, lo_s, hi_s, first_s, nu_s, xs_ref, wgu_ref, bgu_ref, wd_ref, bd_ref, ys_ref):
    w = pl.program_id(0)

    @pl.when(w < nu_s[0])
    def _():
        half = D_MODEL // 2
        x_a, x_b = _unpack_halves(_load_rows(xs_ref))
        h = _dot(x_a, wgu_ref[0, :half, :]) + _dot(x_b, wgu_ref[0, half:, :]) + bgu_ref[0]
        gate = jnp.minimum(h[:, :D_EXPERT], SWIGLU_LIMIT)
        up = jnp.clip(h[:, D_EXPERT:], -SWIGLU_LIMIT, SWIGLU_LIMIT)
        glu = gate * jax.nn.sigmoid(SWIGLU_ALPHA * gate)
        y = _dot(((up + 1.0) * glu).astype(bf16), wd_ref[0]) + bd_ref[0]
        y = y.astype(bf16).astype(f32)
        packed = _pack_halves(y[:, :half], y[:, half:])
        row = lax.broadcasted_iota(jnp.int32, packed.shape, 0)
        mine = (row >= lo_s[w]) & (row < hi_s[w])

        @pl.when(first_s[w] == 1)
        def _():
            _store_rows(ys_ref, jnp.where(mine, packed, jnp.uint32(0)))

        @pl.when(first_s[w] == 0)
        def _():
            _store_rows(ys_ref, jnp.where(mine, packed, _load_rows(ys_ref)))


def _experts(tw, ew, lo, hi, first, n_used, xs, wgu, bgu, wd, bd):
    n_rows = xs.shape[0]
    tm = TM_EXP
    n_items = n_rows // tm + N_EXPERTS - 1
    item = lambda w, nu: jnp.minimum(w, nu[0] - 1)
    tile = lambda w, tw, ew, lo, hi, first, nu: (tw[item(w, nu)], 0, 0)
    wsel = lambda w, tw, ew, lo, hi, first, nu: (ew[item(w, nu)], 0, 0)
    grid_spec = pltpu.PrefetchScalarGridSpec(
        num_scalar_prefetch=6,
        grid=(n_items,),
        in_specs=[pl.BlockSpec((tm,) + ROW_SHAPE, tile),
                  pl.BlockSpec((1, D_MODEL, 2 * D_EXPERT), wsel),
                  pl.BlockSpec((1, 1, 2 * D_EXPERT), wsel),
                  pl.BlockSpec((1, D_EXPERT, D_MODEL), wsel),
                  pl.BlockSpec((1, 1, D_MODEL), wsel)],
        out_specs=pl.BlockSpec((tm,) + ROW_SHAPE, tile),
    )
    return pl.pallas_call(
        _experts_kernel,
        out_shape=jax.ShapeDtypeStruct((n_rows,) + ROW_SHAPE, jnp.uint32),
        grid_spec=grid_spec,
        compiler_params=_params(("arbitrary",)),
        name="experts",
    )(tw, ew, lo, hi, first, n_used, xs, wgu, bgu, wd, bd)


def _combine_kernel(cnt_s, lst_s, dst_s, ri_ref, rw_ref, lstr_ref, x2_ref, g_ref, b_ref, ys_ref, o_ref,
                    yg_ref, sem):
    i = pl.program_id(0)
    slot = i % 2
    rt = x2_ref.shape[0]

    @pl.when(i == 0)
    def _():
        _segment_copies(cnt_s, lst_s, dst_s, i, yg_ref.at[slot], ys_ref, sem.at[slot], False, "start")

    @pl.when(i + 1 < pl.num_programs(0))
    def _():
        _segment_copies(cnt_s, lst_s, dst_s, i + 1, yg_ref.at[1 - slot], ys_ref, sem.at[1 - slot], False, "start")

    idx = ri_ref[...]
    rw = rw_ref[...]
    lane = lax.broadcasted_iota(jnp.int32, (rt, LANES), 1)
    member = jnp.zeros((rt, LANES), f32)
    for k in range(TOP_K):
        member = member + jnp.where(lane == idx[:, k:k + 1], 1.0, 0.0)
    r_ids = lax.broadcasted_iota(jnp.int32, (rt, rt), 0)
    c_ids = lax.broadcasted_iota(jnp.int32, (rt, rt), 1)
    earlier = jnp.where(r_ids > c_ids, 1.0, 0.0).astype(bf16)
    base = lstr_ref[0] + _dot(earlier, member.astype(bf16))
    lane_w = lax.broadcasted_iota(jnp.int32, (rt, TOP_K * rt), 1)
    sel = jnp.zeros((rt, TOP_K * rt), f32)
    for k in range(TOP_K):
        lpos = jnp.sum(jnp.where(lane == idx[:, k:k + 1], base, 0.0), axis=-1, keepdims=True)
        sel = jnp.where(lane_w == lpos.astype(jnp.int32), rw[:, k:k + 1], sel)
    sel = sel.astype(bf16)

    _segment_copies(cnt_s, lst_s, dst_s, i, yg_ref.at[slot], ys_ref, sem.at[slot], False, "wait")
    y_a, y_b = _unpack_halves(_load_rows(yg_ref.at[slot]))
    y = jnp.concatenate([_dot(sel, y_a), _dot(sel, y_b)], axis=1)
    o_ref[...] = _ln(DEEPNORM_ALPHA * x2_ref[...] + y, g_ref[...], b_ref[...])


def _combine(cnt_s, lst_s, dst_s, ri, rw, lst_row, x2, g3, b3, ys):
    t = x2.shape[0]
    row = lambda w: pl.BlockSpec((RT, w), lambda i, *_: (i, 0))
    const = lambda a: pl.BlockSpec(a.shape, lambda i, *_: (0,) * a.ndim)
    grid_spec = pltpu.PrefetchScalarGridSpec(
        num_scalar_prefetch=3,
        grid=(t // RT,),
        in_specs=[row(LANES), row(LANES), pl.BlockSpec((1, 1, LANES), lambda i, *_: (i, 0, 0)),
                  row(D_MODEL), const(g3), const(b3), pl.BlockSpec(memory_space=pl.ANY)],
        out_specs=row(D_MODEL),
        scratch_shapes=[pltpu.VMEM((2, TOP_K * RT) + ROW_SHAPE, jnp.uint32), pltpu.SemaphoreType.DMA((2,))],
    )
    return pl.pallas_call(
        _combine_kernel,
        out_shape=jax.ShapeDtypeStruct((t, D_MODEL), f32),
        grid_spec=grid_spec,
        compiler_params=_params(("arbitrary",)),
        name="combine",
    )(cnt_s, lst_s, dst_s, ri, rw, lst_row, x2, g3, b3, ys)


def _augment_qk(w_qk, w_f, scale):
    d = w_qk.shape[0]
    w_h = (w_qk * scale).reshape(d, FOX_HEADS, FOX_HEAD_DIM)
    f_h = jnp.broadcast_to(w_f[:, :, None], (d, FOX_HEADS, 2 * N_SPLIT))
    pad = jnp.zeros((d, FOX_HEADS, AUG_LANES - FOX_HEAD_DIM - 2 * N_SPLIT), w_qk.dtype)
    return jnp.concatenate([w_h, f_h, pad], axis=-1).reshape(d, FOX_HEADS * AUG_LANES)


def kernel(x, mem, ln_in_g, ln_in_b, w_in, b_forget, ln_v_g, ln_v_b, w_spatial, b_spatial, w_branch_a, w_branch_b, w_out, ln1_g, ln1_b, ln_mem_g, ln_mem_b, w_mq, w_mkv, w_mo, ln2_g, ln2_b, w_router, b_router, w_gate_up, b_gate_up, w_down, b_down, ln3_g, ln3_b):
    bsz, seq, d = x.shape
    m_len = mem.shape[1]
    t = bsz * seq
    assert d == D_MODEL and w_in.shape[0] == DEPTH == 1
    assert seq % TQ == 0 and seq % TM_IN == 0 and seq % TM_POST == 0
    assert t % RT == 0 and RT % TM_POST == 0 and (t * TOP_K) % TM_EXP == 0
    vec = lambda a: a.reshape(1, -1).astype(f32)
    l = 0
    o = [0]
    for w in (GM_WIDTH, GM_WIDTH, FOX_WIDTH, FOX_WIDTH, FOX_WIDTH, FOX_HEADS, D_MODEL, D_MODEL):
        o.append(o[-1] + w)
    wi = w_in[l]
    w_f = wi[:, o[5]:o[6]]
    wuv = wi[:, o[0]:o[2]].astype(bf16)
    wq = _augment_qk(wi[:, o[2]:o[3]], w_f, FOX_HEAD_DIM ** -0.5).astype(bf16)
    wk = _augment_qk(wi[:, o[3]:o[4]], w_f, 1.0).astype(bf16)
    wv = wi[:, o[4]:o[5]].astype(bf16)
    wg = wi[:, o[6]:o[8]].astype(bf16)
    bf_aug = jnp.concatenate(
        [jnp.zeros((FOX_HEADS, FOX_HEAD_DIM), f32),
         jnp.broadcast_to(b_forget[l][:, None], (FOX_HEADS, 2 * N_SPLIT)).astype(f32),
         jnp.zeros((FOX_HEADS, AUG_LANES - FOX_HEAD_DIM - 2 * N_SPLIT), f32)], axis=-1).reshape(1, -1)
    pos = jnp.arange(GM_BLOCK)
    chunk_mask = (pos[None, :] // CHUNK) <= (pos[:, None] // CHUNK)
    wsp = jnp.where(chunk_mask[None], w_spatial[l], 0).astype(bf16)
    bsp = jnp.repeat(b_spatial[l].T, GM_WIDTH // GM_GROUPS, axis=1).astype(f32)
    wr = jnp.pad(w_router[l], ((0, 0), (0, LANES - N_EXPERTS))).astype(f32)
    br = jnp.pad(b_router[l], (0, LANES - N_EXPERTS)).reshape(1, LANES).astype(f32)

    k_mem, v_mem = _mem_kv(mem.reshape(bsz * m_len, d), vec(ln_mem_g[l]), vec(ln_mem_b[l]),
                           w_mkv[l].astype(bf16), m_len)
    xn, gu, vn, q_aug, k_aug, v, sa, sb = _in_proj(
        x.reshape(t, d), vec(ln_in_g), vec(ln_in_b), wuv, wq, wk, wv, wg, bf_aug,
        vec(ln_v_g[l]), vec(ln_v_b[l]), seq)
    yb = _fox_attn(q_aug.reshape(bsz, seq, -1), k_aug.reshape(bsz, seq, -1), v.reshape(bsz, seq, -1))
    x2, x2b, ri, rw, cnt_tiles = _post_mix(
        gu, vn, sa, sb, yb.reshape(t, FOX_WIDTH), xn, k_mem, v_mem, wsp, bsp,
        w_branch_a[l].astype(bf16), w_branch_b[l].astype(bf16), w_out[l].astype(bf16),
        vec(ln1_g[l]), vec(ln1_b[l]), w_mq[l].astype(bf16), w_mo[l].astype(bf16),
        vec(ln2_g[l]), vec(ln2_b[l]), wr, br, seq, m_len)

    i32 = jnp.int32
    cnt = cnt_tiles[:, 0, :N_EXPERTS].astype(i32).reshape(t // RT, RT // TM_POST, N_EXPERTS).sum(axis=1)
    total = cnt.sum(axis=0)
    e_end = jnp.cumsum(total)
    e_off = e_end - total
    dst = e_off[None, :] + jnp.cumsum(cnt, axis=0) - cnt
    lst = jnp.cumsum(cnt, axis=1) - cnt
    lst_pad = jnp.pad(lst, ((0, 0), (0, LANES - N_EXPERTS))).astype(f32)
    flat = lambda a: a.reshape(-1).astype(i32)
    n_items = t * TOP_K // TM_EXP + N_EXPERTS - 1
    first_tile = e_off // TM_EXP
    n_e = jnp.where(total > 0, (e_end - 1) // TM_EXP - first_tile + 1, 0)
    w_end = jnp.cumsum(n_e)
    w_ids = jnp.arange(n_items, dtype=i32)
    e_w = jnp.minimum(jnp.searchsorted(w_end, w_ids, side='right'), N_EXPERTS - 1).astype(i32)
    t_w = jnp.clip(first_tile[e_w] + w_ids - (w_end - n_e)[e_w], 0, t * TOP_K // TM_EXP - 1).astype(i32)
    lo_w = jnp.clip(e_off[e_w] - t_w * TM_EXP, 0, TM_EXP).astype(i32)
    hi_w = jnp.clip(e_end[e_w] - t_w * TM_EXP, 0, TM_EXP).astype(i32)
    first_w = jnp.concatenate([jnp.ones((1,), i32), (t_w[1:] != t_w[:-1]).astype(i32)])
    n_used = w_end[-1:].astype(i32)

    xs = _dispatch(flat(cnt), flat(lst), flat(dst), ri, lst_pad[:, :, None], x2b)
    ys = _experts(t_w, e_w, lo_w, hi_w, first_w, n_used, xs,
                  w_gate_up[l].astype(bf16), b_gate_up[l][:, None, :].astype(f32),
                  w_down[l].astype(bf16), b_down[l][:, None, :].astype(f32))
    out = _combine(flat(cnt), flat(lst), flat(dst), ri, rw, lst_pad[:, None, :], x2, vec(ln3_g[l]), vec(ln3_b[l]), ys)
    return out.reshape(bsz, seq, d)
```

```python
import functools
import math

import jax
import jax.numpy as jnp
from jax import lax
from jax.experimental import pallas as pl
from jax.experimental.pallas import tpu as pltpu

D_MODEL = 1024
CHUNK = 64
GM_WIDTH = 512
GM_GROUPS = 8
GM_BLOCK = 128
FOX_HEADS = 8
FOX_HEAD_DIM = 64
FOX_WIDTH = FOX_HEADS * FOX_HEAD_DIM
MEM_HEADS = 4
MEM_HEAD_DIM = D_MODEL // MEM_HEADS
N_EXPERTS = 32
TOP_K = 4
D_EXPERT = D_MODEL
SWIGLU_ALPHA = 1.702
SWIGLU_LIMIT = 7.0
LN_EPS = 1e-5
DEPTH = 1
DEEPNORM_ALPHA = (2 * DEPTH) ** 0.25

LANES = 128
SUBLANES = 8
AUG_LANES = 2 * FOX_HEAD_DIM
N_SPLIT = 3
VMEM_LIMIT = 56 * 1024 * 1024
NEG_BIG = -1e30
LOG2E = math.log2(math.e)

TM_IN = 256
TQ = 512
TK = 512
ATT_HEADS = 4
VT_ROWS = FOX_HEAD_DIM + 16
TM_POST = 256
RT = 512
TM_EXP = 512
RUN_ROWS = TOP_K * RT + N_EXPERTS * SUBLANES
MAX_RUN_OCTS = RT // SUBLANES

bf16 = jnp.bfloat16
f32 = jnp.float32


def _ln(x, g, b):
    mu = jnp.mean(x, axis=-1, keepdims=True)
    xc = x - mu
    var = jnp.mean(xc * xc, axis=-1, keepdims=True)
    return xc * lax.rsqrt(var + LN_EPS) * g + b


def _dot(a, b):
    return jnp.dot(a, b, preferred_element_type=f32)


def _dot_nt(a, b):
    return lax.dot_general(a, b, (((1,), (1,)), ((), ())), preferred_element_type=f32)


def _params(sem):
    return pltpu.CompilerParams(dimension_semantics=sem, vmem_limit_bytes=VMEM_LIMIT)


def _const_spec(shape):
    nd = len(shape)
    return pl.BlockSpec(shape, lambda *_: (0,) * nd)


def _mem_kv_kernel(mem_ref, g_ref, b_ref, w_ref, k_ref, v_ref):
    mn = _ln(mem_ref[...], g_ref[...], b_ref[...]).astype(bf16)
    kv = _dot(mn, w_ref[...])
    k_ref[...] = kv[:, :D_MODEL].astype(bf16)
    v_ref[...] = kv[:, D_MODEL:].astype(bf16)


def _mem_kv(mem2, g, b, w_mkv, m_len):
    rows = mem2.shape[0]
    return pl.pallas_call(
        _mem_kv_kernel,
        out_shape=(jax.ShapeDtypeStruct((rows, D_MODEL), bf16),) * 2,
        grid=(rows // m_len,),
        in_specs=[pl.BlockSpec((m_len, D_MODEL), lambda i: (i, 0)),
                  _const_spec((1, D_MODEL)), _const_spec((1, D_MODEL)),
                  _const_spec((D_MODEL, 2 * D_MODEL))],
        out_specs=(pl.BlockSpec((m_len, D_MODEL), lambda i: (i, 0)),) * 2,
        compiler_params=_params(("arbitrary",)),
        name="mem_kv",
    )(mem2, g, b, w_mkv)


def _in_proj_kernel(tiles_per_seq, x_ref, g_ref, b_ref, wuv_ref, wq_ref, wk_ref, wvt_ref, vone_ref, wg_ref,
                    bf_ref, lvg_ref, lvb_ref,
                    xn_ref, gu_ref, vn_ref, qt_ref, k_ref, vt_ref, sa_ref, sb_ref, carry_ref):
    tm = x_ref.shape[0]
    xn = _ln(x_ref[...], g_ref[...], b_ref[...])
    xn_ref[...] = xn
    xb = xn.astype(bf16)

    huv = _dot(xb, wuv_ref[...])
    gel = 0.5 * huv * (1.0 + lax.erf(huv * (1.0 / math.sqrt(2.0))))
    gu_ref[...] = gel[:, :GM_WIDTH].astype(bf16)
    vn_ref[...] = _ln(gel[:, GM_WIDTH:], lvg_ref[...], lvb_ref[...]).astype(bf16)

    vt_ref[...] = (_dot_nt(wvt_ref[...], xb) + vone_ref[...]).astype(bf16)

    hg = _dot(xb, wg_ref[...])
    sg = jax.nn.sigmoid(hg)
    sa_ref[...] = sg[:, :D_MODEL].astype(bf16)
    sb_ref[...] = sg[:, D_MODEL:].astype(bf16)

    hq = _dot(xb, wq_ref[...])
    hk = _dot(xb, wk_ref[...])
    lane = lax.broadcasted_iota(jnp.int32, hq.shape, 1) % AUG_LANES
    row = lax.broadcasted_iota(jnp.int32, hq.shape, 0)
    sub = lane - FOX_HEAD_DIM
    in_c = (sub >= 0) & (sub < N_SPLIT)
    in_one = (sub >= N_SPLIT) & (sub < 2 * N_SPLIT)
    z = hq + bf_ref[...]
    log_f = jnp.minimum(z, 0.0) - jnp.log1p(jnp.exp(-jnp.abs(z)))
    c = jnp.where(in_c | in_one, log_f * LOG2E, 0.0)
    shift = 1
    while shift < tm:
        c = c + jnp.where(row >= shift, pltpu.roll(c, shift, 0), 0.0)
        shift *= 2

    @pl.when(pl.program_id(0) % tiles_per_seq == 0)
    def _():
        carry_ref[...] = jnp.zeros_like(carry_ref)

    c = c + carry_ref[...]
    carry_ref[...] = c[tm - 1:tm, :]
    hi = c.astype(bf16).astype(f32)
    r1 = c - hi
    mid = r1.astype(bf16).astype(f32)
    lo = r1 - mid
    piece = jnp.where(sub % N_SPLIT == 0, hi, jnp.where(sub % N_SPLIT == 1, mid, lo))
    is_qk = sub < 0
    q_aug = jnp.where(is_qk, hq, jnp.where(in_c, piece, jnp.where(in_one, 1.0, 0.0)))
    qt_ref[...] = jnp.transpose(q_aug).astype(bf16)
    k_ref[...] = jnp.where(is_qk, hk, jnp.where(in_c, 1.0, jnp.where(in_one, -piece, 0.0))).astype(bf16)


def _in_proj(x2, g, b, wuv, wq, wk, wvt, vone, wg, bf_aug, lvg, lvb, seq):
    t = x2.shape[0]
    tm = TM_IN
    row = lambda w, dt: (jax.ShapeDtypeStruct((t, w), dt), pl.BlockSpec((tm, w), lambda i: (i, 0)))
    col = lambda h, dt: (jax.ShapeDtypeStruct((h, t), dt), pl.BlockSpec((h, tm), lambda i: (0, i)))
    outs = [row(D_MODEL, f32), row(GM_WIDTH, bf16), row(GM_WIDTH, bf16), col(FOX_HEADS * AUG_LANES, bf16),
            row(FOX_HEADS * AUG_LANES, bf16), col(wvt.shape[0], bf16), row(D_MODEL, bf16), row(D_MODEL, bf16)]
    consts = [g, b, wuv, wq, wk, wvt, vone, wg, bf_aug, lvg, lvb]
    return pl.pallas_call(
        functools.partial(_in_proj_kernel, seq // tm),
        out_shape=tuple(o[0] for o in outs),
        grid=(t // tm,),
        in_specs=[pl.BlockSpec((tm, D_MODEL), lambda i: (i, 0))] + [_const_spec(c.shape) for c in consts],
        out_specs=tuple(o[1] for o in outs),
        scratch_shapes=[pltpu.VMEM((1, FOX_HEADS * AUG_LANES), f32)],
        compiler_params=_params(("arbitrary",)),
        name="in_proj",
    )(x2, *consts)


def _fox_attn_kernel(qt_ref, k_ref, vt_ref, o_ref):
    i = pl.program_id(2)
    tq = qt_ref.shape[1]
    key_ids = lax.broadcasted_iota(jnp.int32, (TK, tq), 0)
    qry_ids = lax.broadcasted_iota(jnp.int32, (TK, tq), 1)
    heads = range(ATT_HEADS)
    qts = [qt_ref[h * AUG_LANES:(h + 1) * AUG_LANES, :] for h in heads]

    def step(j, carry, masked):
        start = pl.multiple_of(j * TK, TK)
        sts = [_dot(k_ref[0, pl.ds(start, TK), h * AUG_LANES:(h + 1) * AUG_LANES], qts[h]) for h in heads]
        soft = []
        for h in heads:
            st = jnp.where(key_ids <= qry_ids, sts[h], NEG_BIG) if masked else sts[h]
            m_new = jnp.maximum(carry[h][0], jnp.max(st, axis=0, keepdims=True))
            soft.append((m_new, jnp.exp2(carry[h][0] - m_new), jnp.exp2(st - m_new).astype(bf16)))
        new = []
        for h in heads:
            m_new, alpha, p = soft[h]
            vtj = vt_ref[h * VT_ROWS:(h + 1) * VT_ROWS, pl.ds(start, TK)]
            new.append((m_new, alpha * carry[h][1] + _dot(vtj, p)))
        return tuple(new)

    one = (jnp.full((1, tq), NEG_BIG, f32), jnp.zeros((VT_ROWS, tq), f32))
    carry = lax.fori_loop(0, i, functools.partial(step, masked=False), (one,) * ATT_HEADS)
    carry = step(i, carry, True)
    out_t = jnp.concatenate([acc[:FOX_HEAD_DIM] / acc[FOX_HEAD_DIM:FOX_HEAD_DIM + 1] for _, acc in carry], axis=0)
    o_ref[0] = jnp.transpose(out_t).astype(bf16)


def _fox_attn(qt_aug, k_aug, vt, bsz, s):
    assert TQ == TK
    nq = s // TQ
    return pl.pallas_call(
        _fox_attn_kernel,
        out_shape=jax.ShapeDtypeStruct((bsz, s, FOX_WIDTH), bf16),
        grid=(bsz, FOX_HEADS // ATT_HEADS, nq),
        in_specs=[pl.BlockSpec((ATT_HEADS * AUG_LANES, TQ), lambda bi, hp, i: (hp, bi * nq + i)),
                  pl.BlockSpec((1, s, ATT_HEADS * AUG_LANES), lambda bi, hp, i: (bi, 0, hp)),
                  pl.BlockSpec((ATT_HEADS * VT_ROWS, s), lambda bi, hp, i: (hp, bi))],
        out_specs=pl.BlockSpec((1, TQ, ATT_HEADS * FOX_HEAD_DIM), lambda bi, hp, i: (bi, i, hp)),
        compiler_params=_params(("arbitrary", "arbitrary", "arbitrary")),
        name="fox_attn",
    )(qt_aug, k_aug, vt)


def _post_mix_kernel(gu_ref, vn_ref, sa_ref, sb_ref, yb_ref, xn_ref, km_ref, vm_ref,
                     wsp_ref, bsp_ref, wa_ref, wb_ref, wo_ref, g1_ref, b1_ref,
                     wmq_ref, wmo_ref, g2_ref, b2_ref, wr_ref, br_ref,
                     x2_ref, x2b_ref, ri_ref, rw_ref, cnt_ref):
    tm = gu_ref.shape[0]
    lane = lax.broadcasted_iota(jnp.int32, (GM_BLOCK, LANES), 1)
    blocks = []
    for r in range(tm // GM_BLOCK):
        pairs = []
        for p in range(GM_GROUPS // 2):
            vp = vn_ref[r * GM_BLOCK:(r + 1) * GM_BLOCK, p * LANES:(p + 1) * LANES]
            lo = _dot(wsp_ref[2 * p], vp)
            hi = _dot(wsp_ref[2 * p + 1], vp)
            pairs.append(jnp.where(lane < GM_WIDTH // GM_GROUPS, lo, hi))
        blocks.append(jnp.concatenate(pairs, axis=1) + bsp_ref[...])
    mixed = jnp.concatenate(blocks, axis=0)
    y_a = (gu_ref[...].astype(f32) * mixed).astype(bf16)
    merged = (sa_ref[...].astype(f32) * _dot(y_a, wa_ref[...])
              + sb_ref[...].astype(f32) * _dot(yb_ref[...], wb_ref[...]))
    x1 = _ln(DEEPNORM_ALPHA * xn_ref[...] + _dot(merged.astype(bf16), wo_ref[...]), g1_ref[...], b1_ref[...])
    qm = (_dot(x1.astype(bf16), wmq_ref[...]) * (MEM_HEAD_DIM ** -0.5)).astype(bf16)
    heads = []
    for h in range(MEM_HEADS):
        sl = slice(h * MEM_HEAD_DIM, (h + 1) * MEM_HEAD_DIM)
        s = _dot_nt(qm[:, sl], km_ref[:, sl])
        s = s - jnp.max(s, axis=-1, keepdims=True)
        p = jnp.exp(s)
        p = p / jnp.sum(p, axis=-1, keepdims=True)
        heads.append(_dot(p.astype(bf16), vm_ref[:, sl]))
    o = jnp.concatenate(heads, axis=1).astype(bf16)
    x2 = _ln(DEEPNORM_ALPHA * x1 + _dot(o, wmo_ref[...]), g2_ref[...], b2_ref[...])
    x2_ref[...] = x2
    x2b_ref[...] = x2.astype(bf16)
    logits = jnp.dot(x2, wr_ref[...], precision=lax.Precision.HIGHEST, preferred_element_type=f32) + br_ref[...]
    lane_t = lax.broadcasted_iota(jnp.int32, (tm, LANES), 1)
    work = jnp.where(lane_t < N_EXPERTS, logits, -jnp.inf)
    tops, idxs = [], []
    for _ in range(TOP_K):
        mk = jnp.max(work, axis=-1, keepdims=True)
        ik = jnp.min(jnp.where(work == mk, lane_t, LANES), axis=-1, keepdims=True)
        work = jnp.where(lane_t == ik, -jnp.inf, work)
        tops.append(mk)
        idxs.append(ik)
    exps = [jnp.exp(tk - tops[0]) for tk in tops]
    denom = exps[0] + exps[1] + exps[2] + exps[3]
    member = jnp.zeros((tm, LANES), f32)
    ri = jnp.zeros((tm, LANES), jnp.int32)
    rw = jnp.zeros((tm, LANES), f32)
    for k in range(TOP_K):
        member = member + jnp.where(lane_t == idxs[k], 1.0, 0.0)
        ri = jnp.where(lane_t == k, idxs[k], ri)
        rw = jnp.where(lane_t == k, exps[k] / denom, rw)
    ri_ref[...] = ri
    rw_ref[...] = rw
    cnt_ref[0] = jnp.sum(member, axis=0, keepdims=True)


def _post_mix(gu, vn, sa, sb, yb, xn, k_mem, v_mem, wsp, bsp, wa, wb, wo, g1, b1, wmq, wmo, g2, b2, wr, br,
              seq, m_len):
    t = gu.shape[0]
    tm = TM_POST
    per_seq = seq // tm
    row = lambda w: pl.BlockSpec((tm, w), lambda i: (i, 0))
    memspec = pl.BlockSpec((m_len, D_MODEL), lambda i: (i // per_seq, 0))
    consts = [wsp, bsp, wa, wb, wo, g1, b1, wmq, wmo, g2, b2, wr, br]
    return pl.pallas_call(
        _post_mix_kernel,
        out_shape=(jax.ShapeDtypeStruct((t, D_MODEL), f32),
                   jax.ShapeDtypeStruct((t, D_MODEL), bf16),
                   jax.ShapeDtypeStruct((t, LANES), jnp.int32),
                   jax.ShapeDtypeStruct((t, LANES), f32),
                   jax.ShapeDtypeStruct((t // tm, 1, LANES), f32)),
        grid=(t // tm,),
        in_specs=[row(GM_WIDTH), row(GM_WIDTH), row(D_MODEL), row(D_MODEL), row(FOX_WIDTH), row(D_MODEL),
                  memspec, memspec] + [_const_spec(c.shape) for c in consts],
        out_specs=(row(D_MODEL), row(D_MODEL), row(LANES), row(LANES),
                   pl.BlockSpec((1, 1, LANES), lambda i: (i, 0, 0))),
        compiler_params=_params(("arbitrary",)),
        name="post_mix",
    )(gu, vn, sa, sb, yb, xn, k_mem, v_mem, *consts)


def _run_copies(n_s, lst_s, dst_s, first, n_runs, local_ref, global_ref, sem, to_global, op):
    def per_run(r, _):
        n = n_s[first + r]
        ls = 0 if lst_s is None else lst_s[first + r]
        gd = dst_s[first + r]
        size = MAX_RUN_OCTS
        while size >= 1:
            done = (n // (2 * size)) * (2 * size)

            @pl.when((n & size) != 0)
            def _(size=size, done=done):
                lo = 0 if lst_s is None else pl.multiple_of((ls + done) * SUBLANES, SUBLANES)
                go = pl.multiple_of((gd + done) * SUBLANES, SUBLANES)
                loc = local_ref.at[pl.ds(lo, size * SUBLANES)]
                glo = global_ref.at[pl.ds(go, size * SUBLANES)]
                cp = pltpu.make_async_copy(loc, glo, sem) if to_global else pltpu.make_async_copy(glo, loc, sem)
                cp.start() if op == "start" else cp.wait()

            size //= 2
        return 0

    lax.fori_loop(0, n_runs, per_run, 0)


def _dispatch_kernel(n_s, lst_s, dst_s, zn_s, zdst_s, ri_ref, lstc_ref, x_ref, xs_ref, xg_ref, zero_ref, sem):
    i = pl.program_id(0)
    slot = i % 2
    rt = x_ref.shape[0]
    n_zero = zn_s.shape[0]
    idx_t = jnp.transpose(ri_ref[...].astype(f32))
    sub = lax.broadcasted_iota(jnp.int32, (LANES, rt), 0).astype(f32)
    member_t = jnp.zeros((LANES, rt), f32)
    for k in range(TOP_K):
        member_t = member_t + jnp.where(sub == idx_t[k:k + 1, :], 1.0, 0.0)
    r_ids = lax.broadcasted_iota(jnp.int32, (rt, rt), 0)
    c_ids = lax.broadcasted_iota(jnp.int32, (rt, rt), 1)
    earlier = jnp.where(r_ids < c_ids, 1.0, 0.0).astype(bf16)
    base = lstc_ref[0] + _dot(member_t.astype(bf16), earlier)
    row_i = lax.broadcasted_iota(jnp.int32, (RUN_ROWS, rt), 0)
    sel = jnp.zeros((RUN_ROWS, rt), f32)
    for k in range(TOP_K):
        lpos = jnp.sum(jnp.where(sub == idx_t[k:k + 1, :], base, 0.0), axis=0, keepdims=True)
        sel = jnp.where(row_i == lpos.astype(jnp.int32), 1.0, sel)
    sel = sel.astype(bf16)

    @pl.when(i > 0)
    def _():
        _run_copies(n_s, lst_s, dst_s, (i - 1) * N_EXPERTS, N_EXPERTS, xg_ref.at[1 - slot], xs_ref,
                    sem.at[1 - slot], True, "wait")

    half = D_MODEL // 2
    xg_ref[slot, :, :half] = _dot(sel, x_ref[:, :half])
    xg_ref[slot, :, half:] = _dot(sel, x_ref[:, half:])
    _run_copies(n_s, lst_s, dst_s, i * N_EXPERTS, N_EXPERTS, xg_ref.at[slot], xs_ref, sem.at[slot], True, "start")

    @pl.when(i == pl.num_programs(0) - 1)
    def _():
        zero_ref[...] = jnp.zeros_like(zero_ref)
        _run_copies(zn_s, None, zdst_s, 0, n_zero, zero_ref, xs_ref, sem.at[2], True, "start")
        _run_copies(zn_s, None, zdst_s, 0, n_zero, zero_ref, xs_ref, sem.at[2], True, "wait")
        _run_copies(n_s, lst_s, dst_s, i * N_EXPERTS, N_EXPERTS, xg_ref.at[slot], xs_ref, sem.at[slot], True, "wait")


def _dispatch(n_s, lst_s, dst_s, zn_s, zdst_s, ri, lst_col, x2b, n_rows):
    t = x2b.shape[0]
    grid_spec = pltpu.PrefetchScalarGridSpec(
        num_scalar_prefetch=5,
        grid=(t // RT,),
        in_specs=[pl.BlockSpec((RT, LANES), lambda i, *_: (i, 0)),
                  pl.BlockSpec((1, LANES, 1), lambda i, *_: (i, 0, 0)),
                  pl.BlockSpec((RT, D_MODEL), lambda i, *_: (i, 0))],
        out_specs=pl.BlockSpec(memory_space=pl.ANY),
        scratch_shapes=[pltpu.VMEM((2, RUN_ROWS, D_MODEL), f32), pltpu.VMEM((RT, D_MODEL), f32),
                        pltpu.SemaphoreType.DMA((3,))],
    )
    return pl.pallas_call(
        _dispatch_kernel,
        out_shape=jax.ShapeDtypeStruct((n_rows, D_MODEL), f32),
        grid_spec=grid_spec,
        compiler_params=_params(("arbitrary",)),
        name="dispatch",
    )(n_s, lst_s, dst_s, zn_s, zdst_s, ri, lst_col, x2b)


def _experts_kernel(et_s, newexp_s, nu_s, xs_ref, wgu_ref, bgu_ref, wd_ref, bd_ref, ys_ref, wgu_bf, wd_bf):
    w = pl.program_id(0)

    @pl.when(w < nu_s[0])
    def _():
        @pl.when(newexp_s[w] == 1)
        def _():
            wgu_bf[...] = wgu_ref[0].astype(bf16)
            wd_bf[...] = wd_ref[0].astype(bf16)

        h = _dot(xs_ref[...].astype(bf16), wgu_bf[...]) + bgu_ref[0]
        gate = jnp.minimum(h[:, :D_EXPERT], SWIGLU_LIMIT)
        up = jnp.clip(h[:, D_EXPERT:], -SWIGLU_LIMIT, SWIGLU_LIMIT)
        glu = gate / (1.0 + jnp.exp2(gate * (-SWIGLU_ALPHA * LOG2E)))
        ys_ref[...] = _dot(((up + 1.0) * glu).astype(bf16), wd_bf[...]) + bd_ref[0]

    @pl.when(w >= nu_s[0])
    def _():
        ys_ref[...] = jnp.zeros_like(ys_ref)


def _experts(et, newexp, n_used, xs, wgu, bgu, wd, bd):
    n_rows = xs.shape[0]
    tm = TM_EXP
    used = lambda w, nu: jnp.minimum(w, nu[0] - 1)
    wsel = lambda w, et, newexp, nu: (et[used(w, nu)], 0, 0)
    grid_spec = pltpu.PrefetchScalarGridSpec(
        num_scalar_prefetch=3,
        grid=(n_rows // tm,),
        in_specs=[pl.BlockSpec((tm, D_MODEL), lambda w, et, newexp, nu: (used(w, nu), 0)),
                  pl.BlockSpec((1, D_MODEL, 2 * D_EXPERT), wsel),
                  pl.BlockSpec((1, 1, 2 * D_EXPERT), wsel),
                  pl.BlockSpec((1, D_EXPERT, D_MODEL), wsel),
                  pl.BlockSpec((1, 1, D_MODEL), wsel)],
        out_specs=pl.BlockSpec((tm, D_MODEL), lambda w, et, newexp, nu: (w, 0)),
        scratch_shapes=[pltpu.VMEM((D_MODEL, 2 * D_EXPERT), bf16), pltpu.VMEM((D_EXPERT, D_MODEL), bf16)],
    )
    return pl.pallas_call(
        _experts_kernel,
        out_shape=jax.ShapeDtypeStruct((n_rows, D_MODEL), f32),
        grid_spec=grid_spec,
        compiler_params=_params(("arbitrary",)),
        name="experts",
    )(et, newexp, n_used, xs, wgu, bgu, wd, bd)


def _combine_kernel(n_s, lst_s, dst_s, ri_ref, rw_ref, lstr_ref, x2_ref, g_ref, b_ref, ys_ref, o_ref,
                    yg_ref, sem):
    i = pl.program_id(0)
    slot = i % 2
    rt = x2_ref.shape[0]

    @pl.when(i == 0)
    def _():
        yg_ref[...] = jnp.zeros_like(yg_ref)
        _run_copies(n_s, lst_s, dst_s, i * N_EXPERTS, N_EXPERTS, yg_ref.at[slot], ys_ref, sem.at[slot], False, "start")

    @pl.when(i + 1 < pl.num_programs(0))
    def _():
        _run_copies(n_s, lst_s, dst_s, (i + 1) * N_EXPERTS, N_EXPERTS, yg_ref.at[1 - slot], ys_ref,
                    sem.at[1 - slot], False, "start")

    idx = ri_ref[...]
    rw = rw_ref[...]
    lane = lax.broadcasted_iota(jnp.int32, (rt, LANES), 1)
    member = jnp.zeros((rt, LANES), f32)
    for k in range(TOP_K):
        member = member + jnp.where(lane == idx[:, k:k + 1], 1.0, 0.0)
    r_ids = lax.broadcasted_iota(jnp.int32, (rt, rt), 0)
    c_ids = lax.broadcasted_iota(jnp.int32, (rt, rt), 1)
    earlier = jnp.where(r_ids > c_ids, 1.0, 0.0).astype(bf16)
    base = lstr_ref[0] + _dot(earlier, member.astype(bf16))
    lane_w = lax.broadcasted_iota(jnp.int32, (rt, RUN_ROWS), 1)
    sel = jnp.zeros((rt, RUN_ROWS), f32)
    for k in range(TOP_K):
        lpos = jnp.sum(jnp.where(lane == idx[:, k:k + 1], base, 0.0), axis=-1, keepdims=True)
        sel = jnp.where(lane_w == lpos.astype(jnp.int32), rw[:, k:k + 1], sel)
    sel = sel.astype(bf16)

    _run_copies(n_s, lst_s, dst_s, i * N_EXPERTS, N_EXPERTS, yg_ref.at[slot], ys_ref, sem.at[slot], False, "wait")
    half = D_MODEL // 2
    y = jnp.concatenate([_dot(sel, yg_ref[slot, :, :half].astype(bf16)),
                         _dot(sel, yg_ref[slot, :, half:].astype(bf16))], axis=1)
    o_ref[...] = _ln(DEEPNORM_ALPHA * x2_ref[...] + y, g_ref[...], b_ref[...])


def _combine(n_s, lst_s, dst_s, ri, rw, lst_row, x2, g3, b3, ys):
    t = x2.shape[0]
    row = lambda w: pl.BlockSpec((RT, w), lambda i, *_: (i, 0))
    const = lambda a: pl.BlockSpec(a.shape, lambda i, *_: (0,) * a.ndim)
    grid_spec = pltpu.PrefetchScalarGridSpec(
        num_scalar_prefetch=3,
        grid=(t // RT,),
        in_specs=[row(LANES), row(LANES), pl.BlockSpec((1, 1, LANES), lambda i, *_: (i, 0, 0)),
                  row(D_MODEL), const(g3), const(b3), pl.BlockSpec(memory_space=pl.ANY)],
        out_specs=row(D_MODEL),
        scratch_shapes=[pltpu.VMEM((2, RUN_ROWS, D_MODEL), f32), pltpu.SemaphoreType.DMA((2,))],
    )
    return pl.pallas_call(
        _combine_kernel,
        out_shape=jax.ShapeDtypeStruct((t, D_MODEL), f32),
        grid_spec=grid_spec,
        compiler_params=_params(("arbitrary",)),
        name="combine",
    )(n_s, lst_s, dst_s, ri, rw, lst_row, x2, g3, b3, ys)


def _augment_qk(w_qk, w_f, scale):
    d = w_qk.shape[0]
    w_h = (w_qk * scale).reshape(d, FOX_HEADS, FOX_HEAD_DIM)
    f_h = jnp.broadcast_to(w_f[:, :, None], (d, FOX_HEADS, 2 * N_SPLIT))
    pad = jnp.zeros((d, FOX_HEADS, AUG_LANES - FOX_HEAD_DIM - 2 * N_SPLIT), w_qk.dtype)
    return jnp.concatenate([w_h, f_h, pad], axis=-1).reshape(d, FOX_HEADS * AUG_LANES)


def _round_up(a, m):
    return (a + m - 1) // m * m


def kernel(x, mem, ln_in_g, ln_in_b, w_in, b_forget, ln_v_g, ln_v_b, w_spatial, b_spatial, w_branch_a, w_branch_b, w_out, ln1_g, ln1_b, ln_mem_g, ln_mem_b, w_mq, w_mkv, w_mo, ln2_g, ln2_b, w_router, b_router, w_gate_up, b_gate_up, w_down, b_down, ln3_g, ln3_b):
    bsz, seq, d = x.shape
    m_len = mem.shape[1]
    t = bsz * seq
    assert d == D_MODEL and w_in.shape[0] == DEPTH == 1
    assert seq % TQ == 0 and seq % TM_IN == 0 and seq % TM_POST == 0
    assert t % RT == 0 and RT % TM_POST == 0 and RT == TM_EXP
    vec = lambda a: a.reshape(1, -1).astype(f32)
    l = 0
    o = [0]
    for w in (GM_WIDTH, GM_WIDTH, FOX_WIDTH, FOX_WIDTH, FOX_WIDTH, FOX_HEADS, D_MODEL, D_MODEL):
        o.append(o[-1] + w)
    wi = w_in[l]
    w_f = wi[:, o[5]:o[6]]
    wuv = wi[:, o[0]:o[2]].astype(bf16)
    wq = _augment_qk(wi[:, o[2]:o[3]], w_f, FOX_HEAD_DIM ** -0.5 * LOG2E).astype(bf16)
    wk = _augment_qk(wi[:, o[3]:o[4]], w_f, 1.0).astype(bf16)
    wv_h = wi[:, o[4]:o[5]].T.reshape(FOX_HEADS, FOX_HEAD_DIM, d)
    wvt = jnp.pad(wv_h, ((0, 0), (0, VT_ROWS - FOX_HEAD_DIM), (0, 0))).reshape(FOX_HEADS * VT_ROWS, d).astype(bf16)
    vone = jnp.tile((jnp.arange(VT_ROWS) >= FOX_HEAD_DIM).astype(f32), FOX_HEADS).reshape(-1, 1)
    wg = wi[:, o[6]:o[8]].astype(bf16)
    bf_aug = jnp.concatenate(
        [jnp.zeros((FOX_HEADS, FOX_HEAD_DIM), f32),
         jnp.broadcast_to(b_forget[l][:, None], (FOX_HEADS, 2 * N_SPLIT)).astype(f32),
         jnp.zeros((FOX_HEADS, AUG_LANES - FOX_HEAD_DIM - 2 * N_SPLIT), f32)], axis=-1).reshape(1, -1)
    pos = jnp.arange(GM_BLOCK)
    chunk_mask = (pos[None, :] // CHUNK) <= (pos[:, None] // CHUNK)
    wsp = jnp.where(chunk_mask[None], w_spatial[l], 0).astype(bf16)
    bsp = jnp.repeat(b_spatial[l].T, GM_WIDTH // GM_GROUPS, axis=1).astype(f32)
    wr = jnp.pad(w_router[l], ((0, 0), (0, LANES - N_EXPERTS))).astype(f32)
    br = jnp.pad(b_router[l], (0, LANES - N_EXPERTS)).reshape(1, LANES).astype(f32)

    k_mem, v_mem = _mem_kv(mem.reshape(bsz * m_len, d), vec(ln_mem_g[l]), vec(ln_mem_b[l]),
                           w_mkv[l].astype(bf16), m_len)
    xn, gu, vn, qt_aug, k_aug, vt, sa, sb = _in_proj(
        x.reshape(t, d), vec(ln_in_g), vec(ln_in_b), wuv, wq, wk, wvt, vone, wg, bf_aug,
        vec(ln_v_g[l]), vec(ln_v_b[l]), seq)
    yb = _fox_attn(qt_aug, k_aug.reshape(bsz, seq, -1), vt, bsz, seq)
    x2, x2b, ri, rw, cnt_tiles = _post_mix(
        gu, vn, sa, sb, yb.reshape(t, FOX_WIDTH), xn, k_mem, v_mem, wsp, bsp,
        w_branch_a[l].astype(bf16), w_branch_b[l].astype(bf16), w_out[l].astype(bf16),
        vec(ln1_g[l]), vec(ln1_b[l]), w_mq[l].astype(bf16), w_mo[l].astype(bf16),
        vec(ln2_g[l]), vec(ln2_b[l]), wr, br, seq, m_len)

    i32 = jnp.int32
    n_rt = t // RT
    cnt = cnt_tiles[:, 0, :N_EXPERTS].astype(i32).reshape(n_rt, RT // TM_POST, N_EXPERTS).sum(axis=1)
    cnt8 = _round_up(cnt, SUBLANES)
    lst = jnp.cumsum(cnt8, axis=1) - cnt8
    rows_e = cnt8.sum(axis=0)
    region = _round_up(rows_e, TM_EXP)
    e_end = jnp.cumsum(region)
    e_off = e_end - region
    dst = e_off[None, :] + jnp.cumsum(cnt8, axis=0) - cnt8
    n_tiles = -(-(t * TOP_K + n_rt * N_EXPERTS * (SUBLANES - 1)) // TM_EXP) + N_EXPERTS
    n_used = (e_end[-1:] // TM_EXP).astype(i32)
    tile_ids = jnp.arange(n_tiles, dtype=i32)
    e_t = jnp.minimum(jnp.sum(tile_ids[:, None] * TM_EXP >= e_end[None, :], axis=1), N_EXPERTS - 1).astype(i32)
    newexp = jnp.concatenate([jnp.ones((1,), i32), (e_t[1:] != e_t[:-1]).astype(i32)])
    n_tail = n_tiles - t * TOP_K // TM_EXP
    tail_tile = n_used[0] + jnp.arange(n_tail, dtype=i32)
    z_dst = jnp.concatenate([e_off + rows_e, tail_tile * TM_EXP])
    z_n = jnp.concatenate([region - rows_e, jnp.where(tail_tile < n_tiles, TM_EXP, 0)])
    octs = lambda a: (a.reshape(-1) // SUBLANES).astype(i32)
    lst_pad = jnp.pad(lst, ((0, 0), (0, LANES - N_EXPERTS))).astype(f32)

    xs = _dispatch(octs(cnt8), octs(lst), octs(dst), octs(z_n), octs(z_dst), ri, lst_pad[:, :, None], x2b,
                   n_tiles * TM_EXP)
    ys = _experts(e_t, newexp, n_used, xs, w_gate_up[l], b_gate_up[l][:, None, :].astype(f32),
                  w_down[l], b_down[l][:, None, :].astype(f32))
    out = _combine(octs(cnt8), octs(lst), octs(dst), ri, rw, lst_pad[:, None, :], x2, vec(ln3_g[l]), vec(ln3_b[l]), ys)
    return out.reshape(bsz, seq, d)
```

```python
import functools
import math

import jax
import jax.numpy as jnp
from jax import lax
from jax.experimental import pallas as pl
from jax.experimental.pallas import tpu as pltpu

D_MODEL = 1024
CHUNK = 64
GM_WIDTH = 512
GM_GROUPS = 8
GM_BLOCK = 128
FOX_HEADS = 8
FOX_HEAD_DIM = 64
FOX_WIDTH = FOX_HEADS * FOX_HEAD_DIM
MEM_HEADS = 4
MEM_HEAD_DIM = D_MODEL // MEM_HEADS
N_EXPERTS = 32
TOP_K = 4
D_EXPERT = D_MODEL
SWIGLU_ALPHA = 1.702
SWIGLU_LIMIT = 7.0
LN_EPS = 1e-5
DEPTH = 1
DEEPNORM_ALPHA = (2 * DEPTH) ** 0.25

LANES = 128
SUBLANES = 8
AUG_LANES = 2 * FOX_HEAD_DIM
N_SPLIT = 3
VMEM_LIMIT = 56 * 1024 * 1024
NEG_BIG = -1e30
LOG2E = math.log2(math.e)

TM_IN = 256
TQ = 512
TK = 512
ATT_HEADS = 4
VT_ROWS = FOX_HEAD_DIM + 16
TM_POST = 512
POST_SUB = 256
RT = 512
TM_EXP = 512
RUN_ROWS = TOP_K * RT + N_EXPERTS * SUBLANES
MAX_RUN_OCTS = RT // SUBLANES

bf16 = jnp.bfloat16
f32 = jnp.float32


def _ln(x, g, b):
    mu = jnp.mean(x, axis=-1, keepdims=True)
    xc = x - mu
    var = jnp.mean(xc * xc, axis=-1, keepdims=True)
    return xc * lax.rsqrt(var + LN_EPS) * g + b


def _dot(a, b):
    return jnp.dot(a, b, preferred_element_type=f32)


def _dot_nt(a, b):
    return lax.dot_general(a, b, (((1,), (1,)), ((), ())), preferred_element_type=f32)


def _params(sem):
    return pltpu.CompilerParams(dimension_semantics=sem, vmem_limit_bytes=VMEM_LIMIT)


def _const_spec(shape):
    nd = len(shape)
    return pl.BlockSpec(shape, lambda *_: (0,) * nd)


def _mem_kv_kernel(mem_ref, g_ref, b_ref, w_ref, k_ref, v_ref):
    mn = _ln(mem_ref[...], g_ref[...], b_ref[...]).astype(bf16)
    kv = _dot(mn, w_ref[...])
    k_ref[...] = kv[:, :D_MODEL].astype(bf16)
    v_ref[...] = kv[:, D_MODEL:].astype(bf16)


def _mem_kv(mem2, g, b, w_mkv, m_len):
    rows = mem2.shape[0]
    return pl.pallas_call(
        _mem_kv_kernel,
        out_shape=(jax.ShapeDtypeStruct((rows, D_MODEL), bf16),) * 2,
        grid=(rows // m_len,),
        in_specs=[pl.BlockSpec((m_len, D_MODEL), lambda i: (i, 0)),
                  _const_spec((1, D_MODEL)), _const_spec((1, D_MODEL)),
                  _const_spec((D_MODEL, 2 * D_MODEL))],
        out_specs=(pl.BlockSpec((m_len, D_MODEL), lambda i: (i, 0)),) * 2,
        compiler_params=_params(("arbitrary",)),
        name="mem_kv",
    )(mem2, g, b, w_mkv)


def _in_proj_kernel(tiles_per_seq, x_ref, g_ref, b_ref, wuv_ref, wq_ref, wk_ref, wvt_ref, vone_ref, wg_ref,
                    wf_ref, bf_ref, eq_ref, ek_ref, cq_ref, ck_ref, lvg_ref, lvb_ref,
                    xn_ref, gu_ref, vn_ref, qt_ref, k_ref, vt_ref, sa_ref, sb_ref, carry_ref):
    tm = x_ref.shape[0]
    xn = _ln(x_ref[...], g_ref[...], b_ref[...])
    xn_ref[...] = xn
    xb = xn.astype(bf16)

    huv = _dot(xb, wuv_ref[...])
    gel = 0.5 * huv * (1.0 + lax.erf(huv * (1.0 / math.sqrt(2.0))))
    gu_ref[...] = gel[:, :GM_WIDTH].astype(bf16)
    vn_ref[...] = _ln(gel[:, GM_WIDTH:], lvg_ref[...], lvb_ref[...]).astype(bf16)

    vt_ref[...] = (_dot_nt(wvt_ref[...], xb) + vone_ref[...]).astype(bf16)

    hg = _dot(xb, wg_ref[...])
    sg = jax.nn.sigmoid(hg)
    sa_ref[...] = sg[:, :D_MODEL].astype(bf16)
    sb_ref[...] = sg[:, D_MODEL:].astype(bf16)

    z = _dot(xb, wf_ref[...]) + bf_ref[...]
    lane = lax.broadcasted_iota(jnp.int32, z.shape, 1)
    row = lax.broadcasted_iota(jnp.int32, z.shape, 0)
    log_f = jnp.minimum(z, 0.0) - jnp.log1p(jnp.exp(-jnp.abs(z)))
    c = jnp.where(lane < N_SPLIT * FOX_HEADS, log_f * LOG2E, 0.0)
    shift = 1
    while shift < tm:
        c = c + jnp.where(row >= shift, pltpu.roll(c, shift, 0), 0.0)
        shift *= 2

    @pl.when(pl.program_id(0) % tiles_per_seq == 0)
    def _():
        carry_ref[...] = jnp.zeros_like(carry_ref)

    c = c + carry_ref[...]
    carry_ref[...] = c[tm - 1:tm, :]
    hi = c.astype(bf16).astype(f32)
    r1 = c - hi
    mid = r1.astype(bf16).astype(f32)
    lo = r1 - mid
    piece = jnp.where(lane % N_SPLIT == 0, hi, jnp.where(lane % N_SPLIT == 1, mid, lo)).astype(bf16)
    q_aug = _dot(xb, wq_ref[...]) + _dot(piece, eq_ref[...]) + cq_ref[...]
    qt_ref[...] = jnp.transpose(q_aug).astype(bf16)
    k_ref[...] = (_dot(xb, wk_ref[...]) + _dot(piece, ek_ref[...]) + ck_ref[...]).astype(bf16)


def _in_proj(x2, g, b, wuv, wq, wk, wvt, vone, wg, wf, bf3, eq, ek, cq, ck, lvg, lvb, seq):
    t = x2.shape[0]
    tm = TM_IN
    row = lambda w, dt: (jax.ShapeDtypeStruct((t, w), dt), pl.BlockSpec((tm, w), lambda i: (i, 0)))
    col = lambda h, dt: (jax.ShapeDtypeStruct((h, t), dt), pl.BlockSpec((h, tm), lambda i: (0, i)))
    outs = [row(D_MODEL, f32), row(GM_WIDTH, bf16), row(GM_WIDTH, bf16), col(FOX_HEADS * AUG_LANES, bf16),
            row(FOX_HEADS * AUG_LANES, bf16), col(wvt.shape[0], bf16), row(D_MODEL, bf16), row(D_MODEL, bf16)]
    consts = [g, b, wuv, wq, wk, wvt, vone, wg, wf, bf3, eq, ek, cq, ck, lvg, lvb]
    return pl.pallas_call(
        functools.partial(_in_proj_kernel, seq // tm),
        out_shape=tuple(o[0] for o in outs),
        grid=(t // tm,),
        in_specs=[pl.BlockSpec((tm, D_MODEL), lambda i: (i, 0))] + [_const_spec(c.shape) for c in consts],
        out_specs=tuple(o[1] for o in outs),
        scratch_shapes=[pltpu.VMEM((1, LANES), f32)],
        compiler_params=_params(("arbitrary",)),
        name="in_proj",
    )(x2, *consts)


def _fox_attn_kernel(qt_ref, k_ref, vt_ref, o_ref):
    i = pl.program_id(2)
    tq = qt_ref.shape[1]
    key_ids = lax.broadcasted_iota(jnp.int32, (TK, tq), 0)
    qry_ids = lax.broadcasted_iota(jnp.int32, (TK, tq), 1)
    heads = range(ATT_HEADS)
    qts = [qt_ref[h * AUG_LANES:(h + 1) * AUG_LANES, :] for h in heads]

    def step(j, carry, masked):
        start = pl.multiple_of(j * TK, TK)
        sts = [_dot(k_ref[0, pl.ds(start, TK), h * AUG_LANES:(h + 1) * AUG_LANES], qts[h]) for h in heads]
        soft = []
        for h in heads:
            st = jnp.where(key_ids <= qry_ids, sts[h], NEG_BIG) if masked else sts[h]
            m_new = jnp.maximum(carry[h][0], jnp.max(st, axis=0, keepdims=True))
            soft.append((m_new, jnp.exp2(carry[h][0] - m_new), jnp.exp2(st - m_new).astype(bf16)))
        new = []
        for h in heads:
            m_new, alpha, p = soft[h]
            vtj = vt_ref[h * VT_ROWS:(h + 1) * VT_ROWS, pl.ds(start, TK)]
            new.append((m_new, alpha * carry[h][1] + _dot(vtj, p)))
        return tuple(new)

    one = (jnp.full((1, tq), NEG_BIG, f32), jnp.zeros((VT_ROWS, tq), f32))
    carry = lax.fori_loop(0, i, functools.partial(step, masked=False), (one,) * ATT_HEADS)
    carry = step(i, carry, True)
    out_t = jnp.concatenate([acc[:FOX_HEAD_DIM] / acc[FOX_HEAD_DIM:FOX_HEAD_DIM + 1] for _, acc in carry], axis=0)
    o_ref[0] = jnp.transpose(out_t).astype(bf16)


def _fox_attn(qt_aug, k_aug, vt, bsz, s):
    assert TQ == TK
    nq = s // TQ
    return pl.pallas_call(
        _fox_attn_kernel,
        out_shape=jax.ShapeDtypeStruct((bsz, s, FOX_WIDTH), bf16),
        grid=(bsz, FOX_HEADS // ATT_HEADS, nq),
        in_specs=[pl.BlockSpec((ATT_HEADS * AUG_LANES, TQ), lambda bi, hp, i: (hp, bi * nq + i)),
                  pl.BlockSpec((1, s, ATT_HEADS * AUG_LANES), lambda bi, hp, i: (bi, 0, hp)),
                  pl.BlockSpec((ATT_HEADS * VT_ROWS, s), lambda bi, hp, i: (hp, bi))],
        out_specs=pl.BlockSpec((1, TQ, ATT_HEADS * FOX_HEAD_DIM), lambda bi, hp, i: (bi, i, hp)),
        compiler_params=_params(("arbitrary", "arbitrary", "arbitrary")),
        name="fox_attn",
    )(qt_aug, k_aug, vt)


def _post_mix_kernel(gu_ref, vn_ref, sa_ref, sb_ref, yb_ref, xn_ref, km_ref, vm_ref,
                     wsp_ref, bsp_ref, wa_ref, wb_ref, wo_ref, g1_ref, b1_ref,
                     wmq_ref, wmo_ref, g2_ref, b2_ref, wr_ref, br_ref,
                     x2_ref, x2b_ref, ri_ref, rw_ref, cnt_ref):
    tm = gu_ref.shape[0]
    subs = [slice(s0, s0 + POST_SUB) for s0 in range(0, tm, POST_SUB)]
    each = lambda fn, *lists: [fn(*args) for args in zip(*lists)] if lists else [fn(sl) for sl in subs]
    lane = lax.broadcasted_iota(jnp.int32, (GM_BLOCK, LANES), 1)

    def spatial(sl):
        blocks = []
        for r0 in range(sl.start, sl.stop, GM_BLOCK):
            pairs = []
            for p in range(GM_GROUPS // 2):
                vp = vn_ref[r0:r0 + GM_BLOCK, p * LANES:(p + 1) * LANES]
                lo = _dot(wsp_ref[2 * p], vp)
                hi = _dot(wsp_ref[2 * p + 1], vp)
                pairs.append(jnp.where(lane < GM_WIDTH // GM_GROUPS, lo, hi))
            blocks.append(jnp.concatenate(pairs, axis=1) + bsp_ref[...])
        return (gu_ref[sl, :].astype(f32) * jnp.concatenate(blocks, axis=0)).astype(bf16)

    y_a = each(spatial)
    br_a = each(lambda y: _dot(y, wa_ref[...]), y_a)
    br_b = each(lambda sl: _dot(yb_ref[sl, :], wb_ref[...]))
    merged = each(lambda sl, a, b: (sa_ref[sl, :].astype(f32) * a + sb_ref[sl, :].astype(f32) * b).astype(bf16),
                  subs, br_a, br_b)
    proj = each(lambda m: _dot(m, wo_ref[...]), merged)
    x1 = each(lambda sl, p: _ln(DEEPNORM_ALPHA * xn_ref[sl, :] + p, g1_ref[...], b1_ref[...]), subs, proj)
    qm = each(lambda v: (_dot(v.astype(bf16), wmq_ref[...]) * (MEM_HEAD_DIM ** -0.5)).astype(bf16), x1)
    heads = [[] for _ in subs]
    for h in range(MEM_HEADS):
        sl_h = slice(h * MEM_HEAD_DIM, (h + 1) * MEM_HEAD_DIM)
        sc = each(lambda q: _dot_nt(q[:, sl_h], km_ref[:, sl_h]), qm)
        pr = each(lambda s: jnp.exp(s - jnp.max(s, axis=-1, keepdims=True)), sc)
        pr = each(lambda p: (p / jnp.sum(p, axis=-1, keepdims=True)).astype(bf16), pr)
        for hs, p in zip(heads, pr):
            hs.append(_dot(p, vm_ref[:, sl_h]))
    o = each(lambda hs: jnp.concatenate(hs, axis=1).astype(bf16), heads)
    proj2 = each(lambda v: _dot(v, wmo_ref[...]), o)
    x2 = each(lambda a, p: _ln(DEEPNORM_ALPHA * a + p, g2_ref[...], b2_ref[...]), x1, proj2)

    def route(sl, v):
        x2_ref[sl, :] = v
        x2b_ref[sl, :] = v.astype(bf16)
        v_hi = v.astype(bf16)
        v_lo = (v - v_hi.astype(f32)).astype(bf16)
        both = _dot(v_hi, wr_ref[...])
        logits = both[:, :LANES] + both[:, LANES:] + _dot(v_lo, wr_ref[:, :LANES]) + br_ref[...]
        n = v.shape[0]
        lane_t = lax.broadcasted_iota(jnp.int32, (n, LANES), 1)
        work = jnp.where(lane_t < N_EXPERTS, logits, -jnp.inf)
        tops, idxs = [], []
        for _ in range(TOP_K):
            mk = jnp.max(work, axis=-1, keepdims=True)
            ik = jnp.min(jnp.where(work == mk, lane_t, LANES), axis=-1, keepdims=True)
            work = jnp.where(lane_t == ik, -jnp.inf, work)
            tops.append(mk)
            idxs.append(ik)
        exps = [jnp.exp(tk - tops[0]) for tk in tops]
        denom = exps[0] + exps[1] + exps[2] + exps[3]
        member = jnp.zeros((n, LANES), f32)
        ri = jnp.zeros((n, LANES), jnp.int32)
        rw = jnp.zeros((n, LANES), f32)
        for k in range(TOP_K):
            member = member + jnp.where(lane_t == idxs[k], 1.0, 0.0)
            ri = jnp.where(lane_t == k, idxs[k], ri)
            rw = jnp.where(lane_t == k, exps[k] / denom, rw)
        ri_ref[sl, :] = ri
        rw_ref[sl, :] = rw
        return jnp.sum(member, axis=0, keepdims=True)

    counts = each(route, subs, x2)
    cnt_ref[0] = functools.reduce(lambda a, b: a + b, counts)


def _post_mix(gu, vn, sa, sb, yb, xn, k_mem, v_mem, wsp, bsp, wa, wb, wo, g1, b1, wmq, wmo, g2, b2, wr, br,
              seq, m_len):
    t = gu.shape[0]
    tm = TM_POST
    per_seq = seq // tm
    row = lambda w: pl.BlockSpec((tm, w), lambda i: (i, 0))
    memspec = pl.BlockSpec((m_len, D_MODEL), lambda i: (i // per_seq, 0))
    consts = [wsp, bsp, wa, wb, wo, g1, b1, wmq, wmo, g2, b2, wr, br]
    return pl.pallas_call(
        _post_mix_kernel,
        out_shape=(jax.ShapeDtypeStruct((t, D_MODEL), f32),
                   jax.ShapeDtypeStruct((t, D_MODEL), bf16),
                   jax.ShapeDtypeStruct((t, LANES), jnp.int32),
                   jax.ShapeDtypeStruct((t, LANES), f32),
                   jax.ShapeDtypeStruct((t // tm, 1, LANES), f32)),
        grid=(t // tm,),
        in_specs=[row(GM_WIDTH), row(GM_WIDTH), row(D_MODEL), row(D_MODEL), row(FOX_WIDTH), row(D_MODEL),
                  memspec, memspec] + [_const_spec(c.shape) for c in consts],
        out_specs=(row(D_MODEL), row(D_MODEL), row(LANES), row(LANES),
                   pl.BlockSpec((1, 1, LANES), lambda i: (i, 0, 0))),
        compiler_params=_params(("arbitrary",)),
        name="post_mix",
    )(gu, vn, sa, sb, yb, xn, k_mem, v_mem, *consts)


def _run_copies(n_s, lst_s, dst_s, first, n_runs, local_ref, global_ref, sem, to_global, op):
    def per_run(r, _):
        n = n_s[first + r]
        ls = 0 if lst_s is None else lst_s[first + r]
        gd = dst_s[first + r]
        size = MAX_RUN_OCTS
        while size >= 1:
            done = (n // (2 * size)) * (2 * size)

            @pl.when((n & size) != 0)
            def _(size=size, done=done):
                lo = 0 if lst_s is None else pl.multiple_of((ls + done) * SUBLANES, SUBLANES)
                go = pl.multiple_of((gd + done) * SUBLANES, SUBLANES)
                loc = local_ref.at[pl.ds(lo, size * SUBLANES)]
                glo = global_ref.at[pl.ds(go, size * SUBLANES)]
                cp = pltpu.make_async_copy(loc, glo, sem) if to_global else pltpu.make_async_copy(glo, loc, sem)
                cp.start() if op == "start" else cp.wait()

            size //= 2
        return 0

    lax.fori_loop(0, n_runs, per_run, 0)


def _dispatch_kernel(n_s, lst_s, dst_s, zn_s, zdst_s, ri_ref, lstc_ref, x_ref, xs_ref, xg_ref, zero_ref, sem):
    i = pl.program_id(0)
    slot = i % 2
    rt = x_ref.shape[0]
    n_zero = zn_s.shape[0]
    idx_t = jnp.transpose(ri_ref[...].astype(f32))
    sub = lax.broadcasted_iota(jnp.int32, (LANES, rt), 0).astype(f32)
    member_t = jnp.zeros((LANES, rt), f32)
    for k in range(TOP_K):
        member_t = member_t + jnp.where(sub == idx_t[k:k + 1, :], 1.0, 0.0)
    r_ids = lax.broadcasted_iota(jnp.int32, (rt, rt), 0)
    c_ids = lax.broadcasted_iota(jnp.int32, (rt, rt), 1)
    earlier = jnp.where(r_ids < c_ids, 1.0, 0.0).astype(bf16)
    base = lstc_ref[0] + _dot(member_t.astype(bf16), earlier)
    row_i = lax.broadcasted_iota(jnp.int32, (RUN_ROWS, rt), 0)
    sel = jnp.zeros((RUN_ROWS, rt), f32)
    for k in range(TOP_K):
        lpos = jnp.sum(jnp.where(sub == idx_t[k:k + 1, :], base, 0.0), axis=0, keepdims=True)
        sel = jnp.where(row_i == lpos.astype(jnp.int32), 1.0, sel)
    sel = sel.astype(bf16)

    half = D_MODEL // 2
    xg_ref[slot, :, :half] = _dot(sel, x_ref[:, :half])
    xg_ref[slot, :, half:] = _dot(sel, x_ref[:, half:])
    _run_copies(n_s, lst_s, dst_s, i * N_EXPERTS, N_EXPERTS, xg_ref.at[slot], xs_ref, sem.at[slot], True, "start")

    @pl.when(i > 0)
    def _():
        _run_copies(n_s, lst_s, dst_s, (i - 1) * N_EXPERTS, N_EXPERTS, xg_ref.at[1 - slot], xs_ref,
                    sem.at[1 - slot], True, "wait")

    @pl.when(i == pl.num_programs(0) - 1)
    def _():
        zero_ref[...] = jnp.zeros_like(zero_ref)
        _run_copies(zn_s, None, zdst_s, 0, n_zero, zero_ref, xs_ref, sem.at[2], True, "start")
        _run_copies(zn_s, None, zdst_s, 0, n_zero, zero_ref, xs_ref, sem.at[2], True, "wait")
        _run_copies(n_s, lst_s, dst_s, i * N_EXPERTS, N_EXPERTS, xg_ref.at[slot], xs_ref, sem.at[slot], True, "wait")


def _dispatch(n_s, lst_s, dst_s, zn_s, zdst_s, ri, lst_col, x2b, n_rows):
    t = x2b.shape[0]
    grid_spec = pltpu.PrefetchScalarGridSpec(
        num_scalar_prefetch=5,
        grid=(t // RT,),
        in_specs=[pl.BlockSpec((RT, LANES), lambda i, *_: (i, 0)),
                  pl.BlockSpec((1, LANES, 1), lambda i, *_: (i, 0, 0)),
                  pl.BlockSpec((RT, D_MODEL), lambda i, *_: (i, 0))],
        out_specs=pl.BlockSpec(memory_space=pl.ANY),
        scratch_shapes=[pltpu.VMEM((2, RUN_ROWS, D_MODEL), f32), pltpu.VMEM((RT, D_MODEL), f32),
                        pltpu.SemaphoreType.DMA((3,))],
    )
    return pl.pallas_call(
        _dispatch_kernel,
        out_shape=jax.ShapeDtypeStruct((n_rows, D_MODEL), f32),
        grid_spec=grid_spec,
        compiler_params=_params(("arbitrary",)),
        name="dispatch",
    )(n_s, lst_s, dst_s, zn_s, zdst_s, ri, lst_col, x2b)


def _experts_kernel(et_s, newexp_s, nu_s, xs_ref, wgu_ref, bgu_ref, wd_ref, bd_ref, ys_ref, wgu_bf, wd_bf):
    w = pl.program_id(0)

    @pl.when(w < nu_s[0])
    def _():
        @pl.when(newexp_s[w] == 1)
        def _():
            wgu_bf[...] = wgu_ref[0].astype(bf16)
            wd_bf[...] = wd_ref[0].astype(bf16)

        h = _dot(xs_ref[...].astype(bf16), wgu_bf[...]) + bgu_ref[0]
        gate = jnp.minimum(h[:, :D_EXPERT], SWIGLU_LIMIT)
        up = jnp.clip(h[:, D_EXPERT:], -SWIGLU_LIMIT, SWIGLU_LIMIT)
        glu = gate / (1.0 + jnp.exp2(gate * (-SWIGLU_ALPHA * LOG2E)))
        ys_ref[...] = _dot(((up + 1.0) * glu).astype(bf16), wd_bf[...]) + bd_ref[0]

    @pl.when(w >= nu_s[0])
    def _():
        ys_ref[...] = jnp.zeros_like(ys_ref)


def _experts(et, newexp, n_used, xs, wgu, bgu, wd, bd):
    n_rows = xs.shape[0]
    tm = TM_EXP
    used = lambda w, nu: jnp.minimum(w, nu[0] - 1)
    wsel = lambda w, et, newexp, nu: (et[used(w, nu)], 0, 0)
    grid_spec = pltpu.PrefetchScalarGridSpec(
        num_scalar_prefetch=3,
        grid=(n_rows // tm,),
        in_specs=[pl.BlockSpec((tm, D_MODEL), lambda w, et, newexp, nu: (used(w, nu), 0)),
                  pl.BlockSpec((1, D_MODEL, 2 * D_EXPERT), wsel),
                  pl.BlockSpec((1, 1, 2 * D_EXPERT), wsel),
                  pl.BlockSpec((1, D_EXPERT, D_MODEL), wsel),
                  pl.BlockSpec((1, 1, D_MODEL), wsel)],
        out_specs=pl.BlockSpec((tm, D_MODEL), lambda w, et, newexp, nu: (w, 0)),
        scratch_shapes=[pltpu.VMEM((D_MODEL, 2 * D_EXPERT), bf16), pltpu.VMEM((D_EXPERT, D_MODEL), bf16)],
    )
    return pl.pallas_call(
        _experts_kernel,
        out_shape=jax.ShapeDtypeStruct((n_rows, D_MODEL), f32),
        grid_spec=grid_spec,
        compiler_params=_params(("arbitrary",)),
        name="experts",
    )(et, newexp, n_used, xs, wgu, bgu, wd, bd)


def _combine_kernel(n_s, lst_s, dst_s, ri_ref, rw_ref, lstr_ref, x2_ref, g_ref, b_ref, ys_ref, o_ref,
                    yg_ref, sem):
    i = pl.program_id(0)
    slot = i % 2
    rt = x2_ref.shape[0]

    @pl.when(i == 0)
    def _():
        yg_ref[...] = jnp.zeros_like(yg_ref)
        _run_copies(n_s, lst_s, dst_s, i * N_EXPERTS, N_EXPERTS, yg_ref.at[slot], ys_ref, sem.at[slot], False, "start")

    @pl.when(i + 1 < pl.num_programs(0))
    def _():
        _run_copies(n_s, lst_s, dst_s, (i + 1) * N_EXPERTS, N_EXPERTS, yg_ref.at[1 - slot], ys_ref,
                    sem.at[1 - slot], False, "start")

    _run_copies(n_s, lst_s, dst_s, i * N_EXPERTS, N_EXPERTS, yg_ref.at[slot], ys_ref, sem.at[slot], False, "wait")

    idx = ri_ref[...]
    rw = rw_ref[...]
    lane = lax.broadcasted_iota(jnp.int32, (rt, LANES), 1)
    member = jnp.zeros((rt, LANES), f32)
    for k in range(TOP_K):
        member = member + jnp.where(lane == idx[:, k:k + 1], 1.0, 0.0)
    r_ids = lax.broadcasted_iota(jnp.int32, (rt, rt), 0)
    c_ids = lax.broadcasted_iota(jnp.int32, (rt, rt), 1)
    earlier = jnp.where(r_ids > c_ids, 1.0, 0.0).astype(bf16)
    base = lstr_ref[0] + _dot(earlier, member.astype(bf16))
    lane_w = lax.broadcasted_iota(jnp.int32, (rt, RUN_ROWS), 1)
    sel = jnp.zeros((rt, RUN_ROWS), f32)
    for k in range(TOP_K):
        lpos = jnp.sum(jnp.where(lane == idx[:, k:k + 1], base, 0.0), axis=-1, keepdims=True)
        sel = jnp.where(lane_w == lpos.astype(jnp.int32), rw[:, k:k + 1], sel)
    sel = sel.astype(bf16)

    half = D_MODEL // 2
    y = jnp.concatenate([_dot(sel, yg_ref[slot, :, :half].astype(bf16)),
                         _dot(sel, yg_ref[slot, :, half:].astype(bf16))], axis=1)
    o_ref[...] = _ln(DEEPNORM_ALPHA * x2_ref[...] + y, g_ref[...], b_ref[...])


def _combine(n_s, lst_s, dst_s, ri, rw, lst_row, x2, g3, b3, ys):
    t = x2.shape[0]
    row = lambda w: pl.BlockSpec((RT, w), lambda i, *_: (i, 0))
    const = lambda a: pl.BlockSpec(a.shape, lambda i, *_: (0,) * a.ndim)
    grid_spec = pltpu.PrefetchScalarGridSpec(
        num_scalar_prefetch=3,
        grid=(t // RT,),
        in_specs=[row(LANES), row(LANES), pl.BlockSpec((1, 1, LANES), lambda i, *_: (i, 0, 0)),
                  row(D_MODEL), const(g3), const(b3), pl.BlockSpec(memory_space=pl.ANY)],
        out_specs=row(D_MODEL),
        scratch_shapes=[pltpu.VMEM((2, RUN_ROWS, D_MODEL), f32), pltpu.SemaphoreType.DMA((2,))],
    )
    return pl.pallas_call(
        _combine_kernel,
        out_shape=jax.ShapeDtypeStruct((t, D_MODEL), f32),
        grid_spec=grid_spec,
        compiler_params=_params(("arbitrary",)),
        name="combine",
    )(n_s, lst_s, dst_s, ri, rw, lst_row, x2, g3, b3, ys)


def _augment_qk(w_qk, scale):
    d = w_qk.shape[0]
    w_h = (w_qk * scale).reshape(d, FOX_HEADS, FOX_HEAD_DIM)
    return jnp.pad(w_h, ((0, 0), (0, 0), (0, AUG_LANES - FOX_HEAD_DIM))).reshape(d, FOX_HEADS * AUG_LANES)


def _bias_lane_tables():
    src = jnp.arange(LANES)[:, None]
    dst = jnp.arange(FOX_HEADS * AUG_LANES)[None, :]
    head, off = dst // AUG_LANES, dst % AUG_LANES - FOX_HEAD_DIM
    in_q = (off >= 0) & (off < N_SPLIT)
    in_k = (off >= N_SPLIT) & (off < 2 * N_SPLIT)
    eq = jnp.where(in_q & (src == N_SPLIT * head + off), 1.0, 0.0)
    ek = jnp.where(in_k & (src == N_SPLIT * head + off - N_SPLIT), -1.0, 0.0)
    return eq.astype(bf16), ek.astype(bf16), in_k.astype(f32), in_q.astype(f32)


def _round_up(a, m):
    return (a + m - 1) // m * m


def kernel(x, mem, ln_in_g, ln_in_b, w_in, b_forget, ln_v_g, ln_v_b, w_spatial, b_spatial, w_branch_a, w_branch_b, w_out, ln1_g, ln1_b, ln_mem_g, ln_mem_b, w_mq, w_mkv, w_mo, ln2_g, ln2_b, w_router, b_router, w_gate_up, b_gate_up, w_down, b_down, ln3_g, ln3_b):
    bsz, seq, d = x.shape
    m_len = mem.shape[1]
    t = bsz * seq
    assert d == D_MODEL and w_in.shape[0] == DEPTH == 1
    assert seq % TQ == 0 and seq % TM_IN == 0 and seq % TM_POST == 0
    assert t % RT == 0 and RT % TM_POST == 0 and RT == TM_EXP
    vec = lambda a: a.reshape(1, -1).astype(f32)
    l = 0
    o = [0]
    for w in (GM_WIDTH, GM_WIDTH, FOX_WIDTH, FOX_WIDTH, FOX_WIDTH, FOX_HEADS, D_MODEL, D_MODEL):
        o.append(o[-1] + w)
    wi = w_in[l]
    w_f = wi[:, o[5]:o[6]]
    wuv = wi[:, o[0]:o[2]].astype(bf16)
    wq = _augment_qk(wi[:, o[2]:o[3]], FOX_HEAD_DIM ** -0.5 * LOG2E).astype(bf16)
    wk = _augment_qk(wi[:, o[3]:o[4]], 1.0).astype(bf16)
    wv_h = wi[:, o[4]:o[5]].T.reshape(FOX_HEADS, FOX_HEAD_DIM, d)
    wvt = jnp.pad(wv_h, ((0, 0), (0, VT_ROWS - FOX_HEAD_DIM), (0, 0))).reshape(FOX_HEADS * VT_ROWS, d).astype(bf16)
    vone = jnp.tile((jnp.arange(VT_ROWS) >= FOX_HEAD_DIM).astype(f32), FOX_HEADS).reshape(-1, 1)
    wg = wi[:, o[6]:o[8]].astype(bf16)
    rep3 = lambda a: jnp.pad(jnp.repeat(a, N_SPLIT, axis=-1), ((0, 0), (0, LANES - N_SPLIT * FOX_HEADS)))
    wf = rep3(w_f).astype(bf16)
    bf3 = rep3(b_forget[l][None, :].astype(f32))
    eq, ek, cq, ck = _bias_lane_tables()
    pos = jnp.arange(GM_BLOCK)
    chunk_mask = (pos[None, :] // CHUNK) <= (pos[:, None] // CHUNK)
    wsp = jnp.where(chunk_mask[None], w_spatial[l], 0).astype(bf16)
    bsp = jnp.repeat(b_spatial[l].T, GM_WIDTH // GM_GROUPS, axis=1).astype(f32)
    wr_f = jnp.pad(w_router[l], ((0, 0), (0, LANES - N_EXPERTS))).astype(f32)
    wr_hi = wr_f.astype(bf16)
    wr = jnp.concatenate([wr_hi, (wr_f - wr_hi.astype(f32)).astype(bf16)], axis=1)
    br = jnp.pad(b_router[l], (0, LANES - N_EXPERTS)).reshape(1, LANES).astype(f32)

    k_mem, v_mem = _mem_kv(mem.reshape(bsz * m_len, d), vec(ln_mem_g[l]), vec(ln_mem_b[l]),
                           w_mkv[l].astype(bf16), m_len)
    xn, gu, vn, qt_aug, k_aug, vt, sa, sb = _in_proj(
        x.reshape(t, d), vec(ln_in_g), vec(ln_in_b), wuv, wq, wk, wvt, vone, wg, wf, bf3, eq, ek, cq, ck,
        vec(ln_v_g[l]), vec(ln_v_b[l]), seq)
    yb = _fox_attn(qt_aug, k_aug.reshape(bsz, seq, -1), vt, bsz, seq)
    x2, x2b, ri, rw, cnt_tiles = _post_mix(
        gu, vn, sa, sb, yb.reshape(t, FOX_WIDTH), xn, k_mem, v_mem, wsp, bsp,
        w_branch_a[l].astype(bf16), w_branch_b[l].astype(bf16), w_out[l].astype(bf16),
        vec(ln1_g[l]), vec(ln1_b[l]), w_mq[l].astype(bf16), w_mo[l].astype(bf16),
        vec(ln2_g[l]), vec(ln2_b[l]), wr, br, seq, m_len)

    i32 = jnp.int32
    n_rt = t // RT
    cnt = cnt_tiles[:, 0, :N_EXPERTS].astype(i32).reshape(n_rt, RT // TM_POST, N_EXPERTS).sum(axis=1)
    cnt8 = _round_up(cnt, SUBLANES)
    lst = jnp.cumsum(cnt8, axis=1) - cnt8
    rows_e = cnt8.sum(axis=0)
    region = _round_up(rows_e, TM_EXP)
    e_end = jnp.cumsum(region)
    e_off = e_end - region
    dst = e_off[None, :] + jnp.cumsum(cnt8, axis=0) - cnt8
    n_tiles = -(-(t * TOP_K + n_rt * N_EXPERTS * (SUBLANES - 1)) // TM_EXP) + N_EXPERTS
    n_used = (e_end[-1:] // TM_EXP).astype(i32)
    tile_ids = jnp.arange(n_tiles, dtype=i32)
    e_t = jnp.minimum(jnp.sum(tile_ids[:, None] * TM_EXP >= e_end[None, :], axis=1), N_EXPERTS - 1).astype(i32)
    newexp = jnp.concatenate([jnp.ones((1,), i32), (e_t[1:] != e_t[:-1]).astype(i32)])
    n_tail = n_tiles - t * TOP_K // TM_EXP
    tail_tile = n_used[0] + jnp.arange(n_tail, dtype=i32)
    z_dst = jnp.concatenate([e_off + rows_e, tail_tile * TM_EXP])
    z_n = jnp.concatenate([region - rows_e, jnp.where(tail_tile < n_tiles, TM_EXP, 0)])
    octs = lambda a: (a.reshape(-1) // SUBLANES).astype(i32)
    lst_pad = jnp.pad(lst, ((0, 0), (0, LANES - N_EXPERTS))).astype(f32)

    xs = _dispatch(octs(cnt8), octs(lst), octs(dst), octs(z_n), octs(z_dst), ri, lst_pad[:, :, None], x2b,
                   n_tiles * TM_EXP)
    ys = _experts(e_t, newexp, n_used, xs, w_gate_up[l], b_gate_up[l][:, None, :].astype(f32),
                  w_down[l], b_down[l][:, None, :].astype(f32))
    out = _combine(octs(cnt8), octs(lst), octs(dst), ri, rw, lst_pad[:, None, :], x2, vec(ln3_g[l]), vec(ln3_b[l]), ys)
    return out.reshape(bsz, seq, d)
```

```python
import functools
import math

import jax
import jax.numpy as jnp
from jax import lax
from jax.experimental import pallas as pl
from jax.experimental.pallas import tpu as pltpu

D_MODEL = 1024
CHUNK = 64
GM_WIDTH = 512
GM_GROUPS = 8
GM_BLOCK = 128
FOX_HEADS = 8
FOX_HEAD_DIM = 64
FOX_WIDTH = FOX_HEADS * FOX_HEAD_DIM
MEM_HEADS = 4
MEM_HEAD_DIM = D_MODEL // MEM_HEADS
N_EXPERTS = 32
TOP_K = 4
D_EXPERT = D_MODEL
SWIGLU_ALPHA = 1.702
SWIGLU_LIMIT = 7.0
LN_EPS = 1e-5
DEPTH = 1
DEEPNORM_ALPHA = (2 * DEPTH) ** 0.25

LANES = 128
ROW_ALIGN = 16
AUG_LANES = 2 * FOX_HEAD_DIM
N_SPLIT = 3
VMEM_LIMIT = 56 * 1024 * 1024
NEG_BIG = -1e30
LOG2E = math.log2(math.e)

TM_IN = 256
TQ = 512
TK = 512
ATT_HEADS = 4
VT_ROWS = FOX_HEAD_DIM + 16
TM_POST = 512
POST_SUB = 256
RT = 512
TM_EXP = 512
RUN_ROWS = TOP_K * RT + N_EXPERTS * ROW_ALIGN
MAX_RUN_UNITS = RT // ROW_ALIGN

bf16 = jnp.bfloat16
f32 = jnp.float32


def _ln(x, g, b):
    mu = jnp.mean(x, axis=-1, keepdims=True)
    xc = x - mu
    var = jnp.mean(xc * xc, axis=-1, keepdims=True)
    return xc * lax.rsqrt(var + LN_EPS) * g + b


def _dot(a, b):
    return jnp.dot(a, b, preferred_element_type=f32)


def _dot_nt(a, b):
    return lax.dot_general(a, b, (((1,), (1,)), ((), ())), preferred_element_type=f32)


def _params(sem):
    return pltpu.CompilerParams(dimension_semantics=sem, vmem_limit_bytes=VMEM_LIMIT)


def _const_spec(shape):
    nd = len(shape)
    return pl.BlockSpec(shape, lambda *_: (0,) * nd)


def _mem_kv_kernel(mem_ref, g_ref, b_ref, w_ref, k_ref, v_ref):
    mn = _ln(mem_ref[...], g_ref[...], b_ref[...]).astype(bf16)
    kv = _dot(mn, w_ref[...])
    k_ref[...] = kv[:, :D_MODEL].astype(bf16)
    v_ref[...] = kv[:, D_MODEL:].astype(bf16)


def _mem_kv(mem2, g, b, w_mkv, m_len):
    rows = mem2.shape[0]
    return pl.pallas_call(
        _mem_kv_kernel,
        out_shape=(jax.ShapeDtypeStruct((rows, D_MODEL), bf16),) * 2,
        grid=(rows // m_len,),
        in_specs=[pl.BlockSpec((m_len, D_MODEL), lambda i: (i, 0)),
                  _const_spec((1, D_MODEL)), _const_spec((1, D_MODEL)),
                  _const_spec((D_MODEL, 2 * D_MODEL))],
        out_specs=(pl.BlockSpec((m_len, D_MODEL), lambda i: (i, 0)),) * 2,
        compiler_params=_params(("arbitrary",)),
        name="mem_kv",
    )(mem2, g, b, w_mkv)


def _in_proj_kernel(tiles_per_seq, x_ref, g_ref, b_ref, wuv_ref, wq_ref, wk_ref, wvt_ref, vone_ref, wg_ref,
                    wf_ref, bf_ref, eq_ref, ek_ref, cq_ref, ck_ref, lvg_ref, lvb_ref,
                    xn_ref, gu_ref, vn_ref, qt_ref, k_ref, vt_ref, sa_ref, sb_ref, carry_ref):
    tm = x_ref.shape[0]
    xn = _ln(x_ref[...], g_ref[...], b_ref[...])
    xn_ref[...] = xn
    xb = xn.astype(bf16)

    huv = _dot(xb, wuv_ref[...])
    gel = 0.5 * huv * (1.0 + lax.erf(huv * (1.0 / math.sqrt(2.0))))
    gu_ref[...] = gel[:, :GM_WIDTH].astype(bf16)
    vn_ref[...] = _ln(gel[:, GM_WIDTH:], lvg_ref[...], lvb_ref[...]).astype(bf16)

    vt_ref[...] = (_dot_nt(wvt_ref[...], xb) + vone_ref[...]).astype(bf16)

    hg = _dot(xb, wg_ref[...])
    sg = jax.nn.sigmoid(hg)
    sa_ref[...] = sg[:, :D_MODEL].astype(bf16)
    sb_ref[...] = sg[:, D_MODEL:].astype(bf16)

    z = _dot(xb, wf_ref[...]) + bf_ref[...]
    lane = lax.broadcasted_iota(jnp.int32, z.shape, 1)
    row = lax.broadcasted_iota(jnp.int32, z.shape, 0)
    log_f = jnp.minimum(z, 0.0) - jnp.log1p(jnp.exp(-jnp.abs(z)))
    c = jnp.where(lane < N_SPLIT * FOX_HEADS, log_f * LOG2E, 0.0)
    shift = 1
    while shift < tm:
        c = c + jnp.where(row >= shift, pltpu.roll(c, shift, 0), 0.0)
        shift *= 2

    @pl.when(pl.program_id(0) % tiles_per_seq == 0)
    def _():
        carry_ref[...] = jnp.zeros_like(carry_ref)

    c = c + carry_ref[...]
    carry_ref[...] = c[tm - 1:tm, :]
    hi = c.astype(bf16).astype(f32)
    r1 = c - hi
    mid = r1.astype(bf16).astype(f32)
    lo = r1 - mid
    piece = jnp.where(lane % N_SPLIT == 0, hi, jnp.where(lane % N_SPLIT == 1, mid, lo)).astype(bf16)
    q_aug = _dot(xb, wq_ref[...]) + _dot(piece, eq_ref[...]) + cq_ref[...]
    qt_ref[...] = jnp.transpose(q_aug).astype(bf16)
    k_ref[...] = (_dot(xb, wk_ref[...]) + _dot(piece, ek_ref[...]) + ck_ref[...]).astype(bf16)


def _in_proj(x2, g, b, wuv, wq, wk, wvt, vone, wg, wf, bf3, eq, ek, cq, ck, lvg, lvb, seq):
    t = x2.shape[0]
    tm = TM_IN
    row = lambda w, dt: (jax.ShapeDtypeStruct((t, w), dt), pl.BlockSpec((tm, w), lambda i: (i, 0)))
    col = lambda h, dt: (jax.ShapeDtypeStruct((h, t), dt), pl.BlockSpec((h, tm), lambda i: (0, i)))
    outs = [row(D_MODEL, f32), row(GM_WIDTH, bf16), row(GM_WIDTH, bf16), col(FOX_HEADS * AUG_LANES, bf16),
            row(FOX_HEADS * AUG_LANES, bf16), col(wvt.shape[0], bf16), row(D_MODEL, bf16), row(D_MODEL, bf16)]
    consts = [g, b, wuv, wq, wk, wvt, vone, wg, wf, bf3, eq, ek, cq, ck, lvg, lvb]
    return pl.pallas_call(
        functools.partial(_in_proj_kernel, seq // tm),
        out_shape=tuple(o[0] for o in outs),
        grid=(t // tm,),
        in_specs=[pl.BlockSpec((tm, D_MODEL), lambda i: (i, 0))] + [_const_spec(c.shape) for c in consts],
        out_specs=tuple(o[1] for o in outs),
        scratch_shapes=[pltpu.VMEM((1, LANES), f32)],
        compiler_params=_params(("arbitrary",)),
        name="in_proj",
    )(x2, *consts)


def _fox_attn_kernel(qt_ref, k_ref, vt_ref, o_ref):
    i = pl.program_id(2)
    tq = qt_ref.shape[1]
    key_ids = lax.broadcasted_iota(jnp.int32, (TK, tq), 0)
    qry_ids = lax.broadcasted_iota(jnp.int32, (TK, tq), 1)
    heads = range(ATT_HEADS)
    qts = [qt_ref[h * AUG_LANES:(h + 1) * AUG_LANES, :] for h in heads]

    def step(j, carry, masked):
        start = pl.multiple_of(j * TK, TK)
        sts = [_dot(k_ref[0, pl.ds(start, TK), h * AUG_LANES:(h + 1) * AUG_LANES], qts[h]) for h in heads]
        soft = []
        for h in heads:
            st = jnp.where(key_ids <= qry_ids, sts[h], NEG_BIG) if masked else sts[h]
            m_new = jnp.maximum(carry[h][0], jnp.max(st, axis=0, keepdims=True))
            soft.append((m_new, jnp.exp2(carry[h][0] - m_new), jnp.exp2(st - m_new).astype(bf16)))
        new = []
        for h in heads:
            m_new, alpha, p = soft[h]
            vtj = vt_ref[h * VT_ROWS:(h + 1) * VT_ROWS, pl.ds(start, TK)]
            new.append((m_new, alpha * carry[h][1] + _dot(vtj, p)))
        return tuple(new)

    one = (jnp.full((1, tq), NEG_BIG, f32), jnp.zeros((VT_ROWS, tq), f32))
    carry = lax.fori_loop(0, i, functools.partial(step, masked=False), (one,) * ATT_HEADS)
    carry = step(i, carry, True)
    out_t = jnp.concatenate([acc[:FOX_HEAD_DIM] / acc[FOX_HEAD_DIM:FOX_HEAD_DIM + 1] for _, acc in carry], axis=0)
    o_ref[0] = jnp.transpose(out_t).astype(bf16)


def _fox_attn(qt_aug, k_aug, vt, bsz, s):
    assert TQ == TK
    nq = s // TQ
    return pl.pallas_call(
        _fox_attn_kernel,
        out_shape=jax.ShapeDtypeStruct((bsz, s, FOX_WIDTH), bf16),
        grid=(bsz, FOX_HEADS // ATT_HEADS, nq),
        in_specs=[pl.BlockSpec((ATT_HEADS * AUG_LANES, TQ), lambda bi, hp, i: (hp, bi * nq + i)),
                  pl.BlockSpec((1, s, ATT_HEADS * AUG_LANES), lambda bi, hp, i: (bi, 0, hp)),
                  pl.BlockSpec((ATT_HEADS * VT_ROWS, s), lambda bi, hp, i: (hp, bi))],
        out_specs=pl.BlockSpec((1, TQ, ATT_HEADS * FOX_HEAD_DIM), lambda bi, hp, i: (bi, i, hp)),
        compiler_params=_params(("arbitrary", "arbitrary", "arbitrary")),
        name="fox_attn",
    )(qt_aug, k_aug, vt)


def _post_mix_kernel(gu_ref, vn_ref, sa_ref, sb_ref, yb_ref, xn_ref, km_ref, vm_ref,
                     wsp_ref, bsp_ref, wa_ref, wb_ref, wo_ref, g1_ref, b1_ref,
                     wmq_ref, wmo_ref, g2_ref, b2_ref, wr_ref, br_ref,
                     x2_ref, x2b_ref, ri_ref, rw_ref, cnt_ref):
    tm = gu_ref.shape[0]
    subs = [slice(s0, s0 + POST_SUB) for s0 in range(0, tm, POST_SUB)]
    each = lambda fn, *lists: [fn(*args) for args in zip(*lists)] if lists else [fn(sl) for sl in subs]
    lane = lax.broadcasted_iota(jnp.int32, (GM_BLOCK, LANES), 1)

    def spatial(sl):
        blocks = []
        for r0 in range(sl.start, sl.stop, GM_BLOCK):
            pairs = []
            for p in range(GM_GROUPS // 2):
                vp = vn_ref[r0:r0 + GM_BLOCK, p * LANES:(p + 1) * LANES]
                lo = _dot(wsp_ref[2 * p], vp)
                hi = _dot(wsp_ref[2 * p + 1], vp)
                pairs.append(jnp.where(lane < GM_WIDTH // GM_GROUPS, lo, hi))
            blocks.append(jnp.concatenate(pairs, axis=1) + bsp_ref[...])
        return (gu_ref[sl, :].astype(f32) * jnp.concatenate(blocks, axis=0)).astype(bf16)

    y_a = each(spatial)
    br_a = each(lambda y: _dot(y, wa_ref[...]), y_a)
    br_b = each(lambda sl: _dot(yb_ref[sl, :], wb_ref[...]))
    merged = each(lambda sl, a, b: (sa_ref[sl, :].astype(f32) * a + sb_ref[sl, :].astype(f32) * b).astype(bf16),
                  subs, br_a, br_b)
    proj = each(lambda m: _dot(m, wo_ref[...]), merged)
    x1 = each(lambda sl, p: _ln(DEEPNORM_ALPHA * xn_ref[sl, :] + p, g1_ref[...], b1_ref[...]), subs, proj)
    qm = each(lambda v: (_dot(v.astype(bf16), wmq_ref[...]) * (MEM_HEAD_DIM ** -0.5)).astype(bf16), x1)
    heads = [[] for _ in subs]
    for h in range(MEM_HEADS):
        sl_h = slice(h * MEM_HEAD_DIM, (h + 1) * MEM_HEAD_DIM)
        sc = each(lambda q: _dot_nt(q[:, sl_h], km_ref[:, sl_h]), qm)
        pr = each(lambda s: jnp.exp(s - jnp.max(s, axis=-1, keepdims=True)), sc)
        pr = each(lambda p: (p / jnp.sum(p, axis=-1, keepdims=True)).astype(bf16), pr)
        for hs, p in zip(heads, pr):
            hs.append(_dot(p, vm_ref[:, sl_h]))
    o = each(lambda hs: jnp.concatenate(hs, axis=1).astype(bf16), heads)
    proj2 = each(lambda v: _dot(v, wmo_ref[...]), o)
    x2 = each(lambda a, p: _ln(DEEPNORM_ALPHA * a + p, g2_ref[...], b2_ref[...]), x1, proj2)

    def route(sl, v):
        x2_ref[sl, :] = v
        x2b_ref[sl, :] = v.astype(bf16)
        v_hi = v.astype(bf16)
        v_lo = (v - v_hi.astype(f32)).astype(bf16)
        both = _dot(v_hi, wr_ref[...])
        logits = both[:, :LANES] + both[:, LANES:] + _dot(v_lo, wr_ref[:, :LANES]) + br_ref[...]
        n = v.shape[0]
        lane_t = lax.broadcasted_iota(jnp.int32, (n, LANES), 1)
        work = jnp.where(lane_t < N_EXPERTS, logits, -jnp.inf)
        tops, idxs = [], []
        for _ in range(TOP_K):
            mk = jnp.max(work, axis=-1, keepdims=True)
            ik = jnp.min(jnp.where(work == mk, lane_t, LANES), axis=-1, keepdims=True)
            work = jnp.where(lane_t == ik, -jnp.inf, work)
            tops.append(mk)
            idxs.append(ik)
        exps = [jnp.exp(tk - tops[0]) for tk in tops]
        denom = exps[0] + exps[1] + exps[2] + exps[3]
        member = jnp.zeros((n, LANES), f32)
        ri = jnp.zeros((n, LANES), jnp.int32)
        rw = jnp.zeros((n, LANES), f32)
        for k in range(TOP_K):
            member = member + jnp.where(lane_t == idxs[k], 1.0, 0.0)
            ri = jnp.where(lane_t == k, idxs[k], ri)
            rw = jnp.where(lane_t == k, exps[k] / denom, rw)
        ri_ref[sl, :] = ri
        rw_ref[sl, :] = rw
        return jnp.sum(member, axis=0, keepdims=True)

    counts = each(route, subs, x2)
    cnt_ref[0] = functools.reduce(lambda a, b: a + b, counts)


def _post_mix(gu, vn, sa, sb, yb, xn, k_mem, v_mem, wsp, bsp, wa, wb, wo, g1, b1, wmq, wmo, g2, b2, wr, br,
              seq, m_len):
    t = gu.shape[0]
    tm = TM_POST
    per_seq = seq // tm
    row = lambda w: pl.BlockSpec((tm, w), lambda i: (i, 0))
    memspec = pl.BlockSpec((m_len, D_MODEL), lambda i: (i // per_seq, 0))
    consts = [wsp, bsp, wa, wb, wo, g1, b1, wmq, wmo, g2, b2, wr, br]
    return pl.pallas_call(
        _post_mix_kernel,
        out_shape=(jax.ShapeDtypeStruct((t, D_MODEL), f32),
                   jax.ShapeDtypeStruct((t, D_MODEL), bf16),
                   jax.ShapeDtypeStruct((t, LANES), jnp.int32),
                   jax.ShapeDtypeStruct((t, LANES), f32),
                   jax.ShapeDtypeStruct((t // tm, 1, LANES), f32)),
        grid=(t // tm,),
        in_specs=[row(GM_WIDTH), row(GM_WIDTH), row(D_MODEL), row(D_MODEL), row(FOX_WIDTH), row(D_MODEL),
                  memspec, memspec] + [_const_spec(c.shape) for c in consts],
        out_specs=(row(D_MODEL), row(D_MODEL), row(LANES), row(LANES),
                   pl.BlockSpec((1, 1, LANES), lambda i: (i, 0, 0))),
        compiler_params=_params(("arbitrary",)),
        name="post_mix",
    )(gu, vn, sa, sb, yb, xn, k_mem, v_mem, *consts)


def _run_copies(n_s, lst_s, dst_s, first, n_runs, local_ref, global_ref, sem, to_global, op):
    def per_run(r, _):
        n = n_s[first + r]
        ls = 0 if lst_s is None else lst_s[first + r]
        gd = dst_s[first + r]
        size = MAX_RUN_UNITS
        while size >= 1:
            done = (n // (2 * size)) * (2 * size)

            @pl.when((n & size) != 0)
            def _(size=size, done=done):
                lo = 0 if lst_s is None else pl.multiple_of((ls + done) * ROW_ALIGN, ROW_ALIGN)
                go = pl.multiple_of((gd + done) * ROW_ALIGN, ROW_ALIGN)
                loc = local_ref.at[pl.ds(lo, size * ROW_ALIGN)]
                glo = global_ref.at[pl.ds(go, size * ROW_ALIGN)]
                cp = pltpu.make_async_copy(loc, glo, sem) if to_global else pltpu.make_async_copy(glo, loc, sem)
                cp.start() if op == "start" else cp.wait()

            size //= 2
        return 0

    lax.fori_loop(0, n_runs, per_run, 0)


def _dispatch_kernel(n_s, lst_s, dst_s, zn_s, zdst_s, ri_ref, lstc_ref, x_ref, xs_ref, xg_ref, zero_ref, sem):
    i = pl.program_id(0)
    slot = i % 2
    rt = x_ref.shape[0]
    n_zero = zn_s.shape[0]
    idx_t = jnp.transpose(ri_ref[...].astype(f32))
    sub = lax.broadcasted_iota(jnp.int32, (LANES, rt), 0).astype(f32)
    member_t = jnp.zeros((LANES, rt), f32)
    for k in range(TOP_K):
        member_t = member_t + jnp.where(sub == idx_t[k:k + 1, :], 1.0, 0.0)
    r_ids = lax.broadcasted_iota(jnp.int32, (rt, rt), 0)
    c_ids = lax.broadcasted_iota(jnp.int32, (rt, rt), 1)
    earlier = jnp.where(r_ids < c_ids, 1.0, 0.0).astype(bf16)
    base = lstc_ref[0] + _dot(member_t.astype(bf16), earlier)
    row_i = lax.broadcasted_iota(jnp.int32, (RUN_ROWS, rt), 0)
    sel = jnp.zeros((RUN_ROWS, rt), f32)
    for k in range(TOP_K):
        lpos = jnp.sum(jnp.where(sub == idx_t[k:k + 1, :], base, 0.0), axis=0, keepdims=True)
        sel = jnp.where(row_i == lpos.astype(jnp.int32), 1.0, sel)
    sel = sel.astype(bf16)

    half = D_MODEL // 2
    xg_ref[slot, :, :half] = _dot(sel, x_ref[:, :half]).astype(bf16)
    xg_ref[slot, :, half:] = _dot(sel, x_ref[:, half:]).astype(bf16)
    _run_copies(n_s, lst_s, dst_s, i * N_EXPERTS, N_EXPERTS, xg_ref.at[slot], xs_ref, sem.at[slot], True, "start")

    @pl.when(i > 0)
    def _():
        _run_copies(n_s, lst_s, dst_s, (i - 1) * N_EXPERTS, N_EXPERTS, xg_ref.at[1 - slot], xs_ref,
                    sem.at[1 - slot], True, "wait")

    @pl.when(i == pl.num_programs(0) - 1)
    def _():
        zero_ref[...] = jnp.zeros_like(zero_ref)
        _run_copies(zn_s, None, zdst_s, 0, n_zero, zero_ref, xs_ref, sem.at[2], True, "start")
        _run_copies(zn_s, None, zdst_s, 0, n_zero, zero_ref, xs_ref, sem.at[2], True, "wait")
        _run_copies(n_s, lst_s, dst_s, i * N_EXPERTS, N_EXPERTS, xg_ref.at[slot], xs_ref, sem.at[slot], True, "wait")


def _dispatch(n_s, lst_s, dst_s, zn_s, zdst_s, ri, lst_col, x2b, n_rows):
    t = x2b.shape[0]
    grid_spec = pltpu.PrefetchScalarGridSpec(
        num_scalar_prefetch=5,
        grid=(t // RT,),
        in_specs=[pl.BlockSpec((RT, LANES), lambda i, *_: (i, 0)),
                  pl.BlockSpec((1, LANES, 1), lambda i, *_: (i, 0, 0)),
                  pl.BlockSpec((RT, D_MODEL), lambda i, *_: (i, 0))],
        out_specs=pl.BlockSpec(memory_space=pl.ANY),
        scratch_shapes=[pltpu.VMEM((2, RUN_ROWS, D_MODEL), bf16), pltpu.VMEM((RT, D_MODEL), bf16),
                        pltpu.SemaphoreType.DMA((3,))],
    )
    return pl.pallas_call(
        _dispatch_kernel,
        out_shape=jax.ShapeDtypeStruct((n_rows, D_MODEL), bf16),
        grid_spec=grid_spec,
        compiler_params=_params(("arbitrary",)),
        name="dispatch",
    )(n_s, lst_s, dst_s, zn_s, zdst_s, ri, lst_col, x2b)


def _experts_kernel(et_s, newexp_s, nu_s, xs_ref, wgu_ref, bgu_ref, wd_ref, bd_ref, ys_ref, wgu_bf, wd_bf):
    w = pl.program_id(0)

    @pl.when(w < nu_s[0])
    def _():
        @pl.when(newexp_s[w] == 1)
        def _():
            wgu_bf[...] = wgu_ref[0].astype(bf16)
            wd_bf[...] = wd_ref[0].astype(bf16)

        h = _dot(xs_ref[...], wgu_bf[...]) + bgu_ref[0]
        gate = jnp.minimum(h[:, :D_EXPERT], SWIGLU_LIMIT)
        up = jnp.clip(h[:, D_EXPERT:], -SWIGLU_LIMIT, SWIGLU_LIMIT)
        glu = gate / (1.0 + jnp.exp2(gate * (-SWIGLU_ALPHA * LOG2E)))
        ys_ref[...] = (_dot(((up + 1.0) * glu).astype(bf16), wd_bf[...]) + bd_ref[0]).astype(bf16)

    @pl.when(w >= nu_s[0])
    def _():
        ys_ref[...] = jnp.zeros_like(ys_ref)


def _experts(et, newexp, n_used, xs, wgu, bgu, wd, bd):
    n_rows = xs.shape[0]
    tm = TM_EXP
    used = lambda w, nu: jnp.minimum(w, nu[0] - 1)
    wsel = lambda w, et, newexp, nu: (et[used(w, nu)], 0, 0)
    grid_spec = pltpu.PrefetchScalarGridSpec(
        num_scalar_prefetch=3,
        grid=(n_rows // tm,),
        in_specs=[pl.BlockSpec((tm, D_MODEL), lambda w, et, newexp, nu: (used(w, nu), 0)),
                  pl.BlockSpec((1, D_MODEL, 2 * D_EXPERT), wsel),
                  pl.BlockSpec((1, 1, 2 * D_EXPERT), wsel),
                  pl.BlockSpec((1, D_EXPERT, D_MODEL), wsel),
                  pl.BlockSpec((1, 1, D_MODEL), wsel)],
        out_specs=pl.BlockSpec((tm, D_MODEL), lambda w, et, newexp, nu: (w, 0)),
        scratch_shapes=[pltpu.VMEM((D_MODEL, 2 * D_EXPERT), bf16), pltpu.VMEM((D_EXPERT, D_MODEL), bf16)],
    )
    return pl.pallas_call(
        _experts_kernel,
        out_shape=jax.ShapeDtypeStruct((n_rows, D_MODEL), bf16),
        grid_spec=grid_spec,
        compiler_params=_params(("arbitrary",)),
        name="experts",
    )(et, newexp, n_used, xs, wgu, bgu, wd, bd)


def _combine_kernel(n_s, lst_s, dst_s, ri_ref, rw_ref, lstr_ref, x2_ref, g_ref, b_ref, ys_ref, o_ref,
                    yg_ref, sem):
    i = pl.program_id(0)
    slot = i % 2
    rt = x2_ref.shape[0]

    @pl.when(i == 0)
    def _():
        yg_ref[...] = jnp.zeros_like(yg_ref)
        _run_copies(n_s, lst_s, dst_s, i * N_EXPERTS, N_EXPERTS, yg_ref.at[slot], ys_ref, sem.at[slot], False, "start")

    @pl.when(i + 1 < pl.num_programs(0))
    def _():
        _run_copies(n_s, lst_s, dst_s, (i + 1) * N_EXPERTS, N_EXPERTS, yg_ref.at[1 - slot], ys_ref,
                    sem.at[1 - slot], False, "start")

    _run_copies(n_s, lst_s, dst_s, i * N_EXPERTS, N_EXPERTS, yg_ref.at[slot], ys_ref, sem.at[slot], False, "wait")

    idx = ri_ref[...]
    rw = rw_ref[...]
    lane = lax.broadcasted_iota(jnp.int32, (rt, LANES), 1)
    member = jnp.zeros((rt, LANES), f32)
    for k in range(TOP_K):
        member = member + jnp.where(lane == idx[:, k:k + 1], 1.0, 0.0)
    r_ids = lax.broadcasted_iota(jnp.int32, (rt, rt), 0)
    c_ids = lax.broadcasted_iota(jnp.int32, (rt, rt), 1)
    earlier = jnp.where(r_ids > c_ids, 1.0, 0.0).astype(bf16)
    base = lstr_ref[0] + _dot(earlier, member.astype(bf16))
    lane_w = lax.broadcasted_iota(jnp.int32, (rt, RUN_ROWS), 1)
    sel = jnp.zeros((rt, RUN_ROWS), f32)
    for k in range(TOP_K):
        lpos = jnp.sum(jnp.where(lane == idx[:, k:k + 1], base, 0.0), axis=-1, keepdims=True)
        sel = jnp.where(lane_w == lpos.astype(jnp.int32), rw[:, k:k + 1], sel)
    sel = sel.astype(bf16)

    y = _dot(sel, yg_ref[slot])
    o_ref[...] = _ln(DEEPNORM_ALPHA * x2_ref[...] + y, g_ref[...], b_ref[...])


def _combine(n_s, lst_s, dst_s, ri, rw, lst_row, x2, g3, b3, ys):
    t = x2.shape[0]
    row = lambda w: pl.BlockSpec((RT, w), lambda i, *_: (i, 0))
    const = lambda a: pl.BlockSpec(a.shape, lambda i, *_: (0,) * a.ndim)
    grid_spec = pltpu.PrefetchScalarGridSpec(
        num_scalar_prefetch=3,
        grid=(t // RT,),
        in_specs=[row(LANES), row(LANES), pl.BlockSpec((1, 1, LANES), lambda i, *_: (i, 0, 0)),
                  row(D_MODEL), const(g3), const(b3), pl.BlockSpec(memory_space=pl.ANY)],
        out_specs=row(D_MODEL),
        scratch_shapes=[pltpu.VMEM((2, RUN_ROWS, D_MODEL), bf16), pltpu.SemaphoreType.DMA((2,))],
    )
    return pl.pallas_call(
        _combine_kernel,
        out_shape=jax.ShapeDtypeStruct((t, D_MODEL), f32),
        grid_spec=grid_spec,
        compiler_params=_params(("arbitrary",)),
        name="combine",
    )(n_s, lst_s, dst_s, ri, rw, lst_row, x2, g3, b3, ys)


def _augment_qk(w_qk, scale):
    d = w_qk.shape[0]
    w_h = (w_qk * scale).reshape(d, FOX_HEADS, FOX_HEAD_DIM)
    return jnp.pad(w_h, ((0, 0), (0, 0), (0, AUG_LANES - FOX_HEAD_DIM))).reshape(d, FOX_HEADS * AUG_LANES)


def _bias_lane_tables():
    src = jnp.arange(LANES)[:, None]
    dst = jnp.arange(FOX_HEADS * AUG_LANES)[None, :]
    head, off = dst // AUG_LANES, dst % AUG_LANES - FOX_HEAD_DIM
    in_q = (off >= 0) & (off < N_SPLIT)
    in_k = (off >= N_SPLIT) & (off < 2 * N_SPLIT)
    eq = jnp.where(in_q & (src == N_SPLIT * head + off), 1.0, 0.0)
    ek = jnp.where(in_k & (src == N_SPLIT * head + off - N_SPLIT), -1.0, 0.0)
    return eq.astype(bf16), ek.astype(bf16), in_k.astype(f32), in_q.astype(f32)


def _round_up(a, m):
    return (a + m - 1) // m * m


def kernel(x, mem, ln_in_g, ln_in_b, w_in, b_forget, ln_v_g, ln_v_b, w_spatial, b_spatial, w_branch_a, w_branch_b, w_out, ln1_g, ln1_b, ln_mem_g, ln_mem_b, w_mq, w_mkv, w_mo, ln2_g, ln2_b, w_router, b_router, w_gate_up, b_gate_up, w_down, b_down, ln3_g, ln3_b):
    bsz, seq, d = x.shape
    m_len = mem.shape[1]
    t = bsz * seq
    assert d == D_MODEL and w_in.shape[0] == DEPTH == 1
    assert seq % TQ == 0 and seq % TM_IN == 0 and seq % TM_POST == 0
    assert t % RT == 0 and RT % TM_POST == 0 and RT == TM_EXP
    vec = lambda a: a.reshape(1, -1).astype(f32)
    l = 0
    o = [0]
    for w in (GM_WIDTH, GM_WIDTH, FOX_WIDTH, FOX_WIDTH, FOX_WIDTH, FOX_HEADS, D_MODEL, D_MODEL):
        o.append(o[-1] + w)
    wi = w_in[l]
    w_f = wi[:, o[5]:o[6]]
    wuv = wi[:, o[0]:o[2]].astype(bf16)
    wq = _augment_qk(wi[:, o[2]:o[3]], FOX_HEAD_DIM ** -0.5 * LOG2E).astype(bf16)
    wk = _augment_qk(wi[:, o[3]:o[4]], 1.0).astype(bf16)
    wv_h = wi[:, o[4]:o[5]].T.reshape(FOX_HEADS, FOX_HEAD_DIM, d)
    wvt = jnp.pad(wv_h, ((0, 0), (0, VT_ROWS - FOX_HEAD_DIM), (0, 0))).reshape(FOX_HEADS * VT_ROWS, d).astype(bf16)
    vone = jnp.tile((jnp.arange(VT_ROWS) >= FOX_HEAD_DIM).astype(f32), FOX_HEADS).reshape(-1, 1)
    wg = wi[:, o[6]:o[8]].astype(bf16)
    rep3 = lambda a: jnp.pad(jnp.repeat(a, N_SPLIT, axis=-1), ((0, 0), (0, LANES - N_SPLIT * FOX_HEADS)))
    wf = rep3(w_f).astype(bf16)
    bf3 = rep3(b_forget[l][None, :].astype(f32))
    eq, ek, cq, ck = _bias_lane_tables()
    pos = jnp.arange(GM_BLOCK)
    chunk_mask = (pos[None, :] // CHUNK) <= (pos[:, None] // CHUNK)
    wsp = jnp.where(chunk_mask[None], w_spatial[l], 0).astype(bf16)
    bsp = jnp.repeat(b_spatial[l].T, GM_WIDTH // GM_GROUPS, axis=1).astype(f32)
    wr_f = jnp.pad(w_router[l], ((0, 0), (0, LANES - N_EXPERTS))).astype(f32)
    wr_hi = wr_f.astype(bf16)
    wr = jnp.concatenate([wr_hi, (wr_f - wr_hi.astype(f32)).astype(bf16)], axis=1)
    br = jnp.pad(b_router[l], (0, LANES - N_EXPERTS)).reshape(1, LANES).astype(f32)

    k_mem, v_mem = _mem_kv(mem.reshape(bsz * m_len, d), vec(ln_mem_g[l]), vec(ln_mem_b[l]),
                           w_mkv[l].astype(bf16), m_len)
    xn, gu, vn, qt_aug, k_aug, vt, sa, sb = _in_proj(
        x.reshape(t, d), vec(ln_in_g), vec(ln_in_b), wuv, wq, wk, wvt, vone, wg, wf, bf3, eq, ek, cq, ck,
        vec(ln_v_g[l]), vec(ln_v_b[l]), seq)
    yb = _fox_attn(qt_aug, k_aug.reshape(bsz, seq, -1), vt, bsz, seq)
    x2, x2b, ri, rw, cnt_tiles = _post_mix(
        gu, vn, sa, sb, yb.reshape(t, FOX_WIDTH), xn, k_mem, v_mem, wsp, bsp,
        w_branch_a[l].astype(bf16), w_branch_b[l].astype(bf16), w_out[l].astype(bf16),
        vec(ln1_g[l]), vec(ln1_b[l]), w_mq[l].astype(bf16), w_mo[l].astype(bf16),
        vec(ln2_g[l]), vec(ln2_b[l]), wr, br, seq, m_len)

    i32 = jnp.int32
    n_rt = t // RT
    cnt = cnt_tiles[:, 0, :N_EXPERTS].astype(i32).reshape(n_rt, RT // TM_POST, N_EXPERTS).sum(axis=1)
    cnt8 = _round_up(cnt, ROW_ALIGN)
    lst = jnp.cumsum(cnt8, axis=1) - cnt8
    rows_e = cnt8.sum(axis=0)
    region = _round_up(rows_e, TM_EXP)
    e_end = jnp.cumsum(region)
    e_off = e_end - region
    dst = e_off[None, :] + jnp.cumsum(cnt8, axis=0) - cnt8
    n_tiles = -(-(t * TOP_K + n_rt * N_EXPERTS * (ROW_ALIGN - 1)) // TM_EXP) + N_EXPERTS
    n_used = (e_end[-1:] // TM_EXP).astype(i32)
    tile_ids = jnp.arange(n_tiles, dtype=i32)
    e_t = jnp.minimum(jnp.sum(tile_ids[:, None] * TM_EXP >= e_end[None, :], axis=1), N_EXPERTS - 1).astype(i32)
    newexp = jnp.concatenate([jnp.ones((1,), i32), (e_t[1:] != e_t[:-1]).astype(i32)])
    n_tail = n_tiles - t * TOP_K // TM_EXP
    tail_tile = n_used[0] + jnp.arange(n_tail, dtype=i32)
    z_dst = jnp.concatenate([e_off + rows_e, tail_tile * TM_EXP])
    z_n = jnp.concatenate([region - rows_e, jnp.where(tail_tile < n_tiles, TM_EXP, 0)])
    octs = lambda a: (a.reshape(-1) // ROW_ALIGN).astype(i32)
    lst_pad = jnp.pad(lst, ((0, 0), (0, LANES - N_EXPERTS))).astype(f32)

    xs = _dispatch(octs(cnt8), octs(lst), octs(dst), octs(z_n), octs(z_dst), ri, lst_pad[:, :, None], x2b,
                   n_tiles * TM_EXP)
    ys = _experts(e_t, newexp, n_used, xs, w_gate_up[l], b_gate_up[l][:, None, :].astype(f32),
                  w_down[l], b_down[l][:, None, :].astype(f32))
    out = _combine(octs(cnt8), octs(lst), octs(dst), ri, rw, lst_pad[:, None, :], x2, vec(ln3_g[l]), vec(ln3_b[l]), ys)
    return out.reshape(bsz, seq, d)
```

```python
import functools
import math

import jax
import jax.numpy as jnp
from jax import lax
from jax.experimental import pallas as pl
from jax.experimental.pallas import tpu as pltpu

D_MODEL = 1024
CHUNK = 64
GM_WIDTH = 512
GM_GROUPS = 8
GM_BLOCK = 128
FOX_HEADS = 8
FOX_HEAD_DIM = 64
FOX_WIDTH = FOX_HEADS * FOX_HEAD_DIM
MEM_HEADS = 4
MEM_HEAD_DIM = D_MODEL // MEM_HEADS
N_EXPERTS = 32
TOP_K = 4
D_EXPERT = D_MODEL
SWIGLU_ALPHA = 1.702
SWIGLU_LIMIT = 7.0
LN_EPS = 1e-5
DEPTH = 1
DEEPNORM_ALPHA = (2 * DEPTH) ** 0.25

LANES = 128
ROW_ALIGN = 16
AUG_LANES = 2 * FOX_HEAD_DIM
N_SPLIT = 3
VMEM_LIMIT = 56 * 1024 * 1024
NEG_BIG = -1e30
LOG2E = math.log2(math.e)

TM_IN = 256
TQ = 512
TK = 512
ATT_HEADS = 4
VT_ROWS = FOX_HEAD_DIM + 16
TM_POST = 512
POST_SUB = 256
RT = 512
TM_EXP = 512
RUN_ROWS = TOP_K * RT + N_EXPERTS * ROW_ALIGN
MAX_RUN_UNITS = RT // ROW_ALIGN
PIECE_SIZES = tuple(MAX_RUN_UNITS >> b for b in range(MAX_RUN_UNITS.bit_length()))

bf16 = jnp.bfloat16
f32 = jnp.float32


def _ln(x, g, b):
    mu = jnp.mean(x, axis=-1, keepdims=True)
    xc = x - mu
    var = jnp.mean(xc * xc, axis=-1, keepdims=True)
    return xc * lax.rsqrt(var + LN_EPS) * g + b


def _dot(a, b):
    return jnp.dot(a, b, preferred_element_type=f32)


def _dot_nt(a, b):
    return lax.dot_general(a, b, (((1,), (1,)), ((), ())), preferred_element_type=f32)


def _params(sem):
    return pltpu.CompilerParams(dimension_semantics=sem, vmem_limit_bytes=VMEM_LIMIT)


def _const_spec(shape):
    nd = len(shape)
    return pl.BlockSpec(shape, lambda *_: (0,) * nd)


def _mem_kv_kernel(mem_ref, g_ref, b_ref, w_ref, k_ref, v_ref):
    mn = _ln(mem_ref[...], g_ref[...], b_ref[...]).astype(bf16)
    kv = _dot(mn, w_ref[...])
    k_ref[...] = kv[:, :D_MODEL].astype(bf16)
    v_ref[...] = kv[:, D_MODEL:].astype(bf16)


def _mem_kv(mem2, g, b, w_mkv, m_len):
    rows = mem2.shape[0]
    return pl.pallas_call(
        _mem_kv_kernel,
        out_shape=(jax.ShapeDtypeStruct((rows, D_MODEL), bf16),) * 2,
        grid=(rows // m_len,),
        in_specs=[pl.BlockSpec((m_len, D_MODEL), lambda i: (i, 0)),
                  _const_spec((1, D_MODEL)), _const_spec((1, D_MODEL)),
                  _const_spec((D_MODEL, 2 * D_MODEL))],
        out_specs=(pl.BlockSpec((m_len, D_MODEL), lambda i: (i, 0)),) * 2,
        compiler_params=_params(("arbitrary",)),
        name="mem_kv",
    )(mem2, g, b, w_mkv)


def _in_proj_kernel(tiles_per_seq, x_ref, g_ref, b_ref, wuv_ref, wq_ref, wk_ref, wvt_ref, vone_ref, wg_ref,
                    wf_ref, bf_ref, eq_ref, ek_ref, cq_ref, ck_ref, lvg_ref, lvb_ref,
                    xn_ref, gu_ref, vn_ref, qt_ref, k_ref, vt_ref, sa_ref, sb_ref, carry_ref):
    tm = x_ref.shape[0]
    xn = _ln(x_ref[...], g_ref[...], b_ref[...])
    xn_ref[...] = xn
    xb = xn.astype(bf16)

    huv = _dot(xb, wuv_ref[...])
    gel = 0.5 * huv * (1.0 + lax.erf(huv * (1.0 / math.sqrt(2.0))))
    gu_ref[...] = gel[:, :GM_WIDTH].astype(bf16)
    vn_ref[...] = _ln(gel[:, GM_WIDTH:], lvg_ref[...], lvb_ref[...]).astype(bf16)

    vt_ref[...] = (_dot_nt(wvt_ref[...], xb) + vone_ref[...]).astype(bf16)

    hg = _dot(xb, wg_ref[...])
    sg = jax.nn.sigmoid(hg)
    sa_ref[...] = sg[:, :D_MODEL].astype(bf16)
    sb_ref[...] = sg[:, D_MODEL:].astype(bf16)

    z = _dot(xb, wf_ref[...]) + bf_ref[...]
    lane = lax.broadcasted_iota(jnp.int32, z.shape, 1)
    row = lax.broadcasted_iota(jnp.int32, z.shape, 0)
    log_f = jnp.minimum(z, 0.0) - jnp.log1p(jnp.exp(-jnp.abs(z)))
    c = jnp.where(lane < N_SPLIT * FOX_HEADS, log_f * LOG2E, 0.0)
    shift = 1
    while shift < tm:
        c = c + jnp.where(row >= shift, pltpu.roll(c, shift, 0), 0.0)
        shift *= 2

    @pl.when(pl.program_id(0) % tiles_per_seq == 0)
    def _():
        carry_ref[...] = jnp.zeros_like(carry_ref)

    c = c + carry_ref[...]
    carry_ref[...] = c[tm - 1:tm, :]
    hi = c.astype(bf16).astype(f32)
    r1 = c - hi
    mid = r1.astype(bf16).astype(f32)
    lo = r1 - mid
    piece = jnp.where(lane % N_SPLIT == 0, hi, jnp.where(lane % N_SPLIT == 1, mid, lo)).astype(bf16)
    q_aug = _dot(xb, wq_ref[...]) + _dot(piece, eq_ref[...]) + cq_ref[...]
    qt_ref[...] = jnp.transpose(q_aug).astype(bf16)
    k_ref[...] = (_dot(xb, wk_ref[...]) + _dot(piece, ek_ref[...]) + ck_ref[...]).astype(bf16)


def _in_proj(x2, g, b, wuv, wq, wk, wvt, vone, wg, wf, bf3, eq, ek, cq, ck, lvg, lvb, seq):
    t = x2.shape[0]
    tm = TM_IN
    row = lambda w, dt: (jax.ShapeDtypeStruct((t, w), dt), pl.BlockSpec((tm, w), lambda i: (i, 0)))
    col = lambda h, dt: (jax.ShapeDtypeStruct((h, t), dt), pl.BlockSpec((h, tm), lambda i: (0, i)))
    outs = [row(D_MODEL, f32), row(GM_WIDTH, bf16), row(GM_WIDTH, bf16), col(FOX_HEADS * AUG_LANES, bf16),
            row(FOX_HEADS * AUG_LANES, bf16), col(wvt.shape[0], bf16), row(D_MODEL, bf16), row(D_MODEL, bf16)]
    consts = [g, b, wuv, wq, wk, wvt, vone, wg, wf, bf3, eq, ek, cq, ck, lvg, lvb]
    return pl.pallas_call(
        functools.partial(_in_proj_kernel, seq // tm),
        out_shape=tuple(o[0] for o in outs),
        grid=(t // tm,),
        in_specs=[pl.BlockSpec((tm, D_MODEL), lambda i: (i, 0))] + [_const_spec(c.shape) for c in consts],
        out_specs=tuple(o[1] for o in outs),
        scratch_shapes=[pltpu.VMEM((1, LANES), f32)],
        compiler_params=_params(("arbitrary",)),
        name="in_proj",
    )(x2, *consts)


def _fox_attn_kernel(qt_ref, k_ref, vt_ref, o_ref):
    i = pl.program_id(2)
    tq = qt_ref.shape[1]
    key_ids = lax.broadcasted_iota(jnp.int32, (TK, tq), 0)
    qry_ids = lax.broadcasted_iota(jnp.int32, (TK, tq), 1)
    heads = range(ATT_HEADS)
    qts = [qt_ref[h * AUG_LANES:(h + 1) * AUG_LANES, :] for h in heads]

    def step(j, carry, masked):
        start = pl.multiple_of(j * TK, TK)
        sts = [_dot(k_ref[0, pl.ds(start, TK), h * AUG_LANES:(h + 1) * AUG_LANES], qts[h]) for h in heads]
        soft = []
        for h in heads:
            st = jnp.where(key_ids <= qry_ids, sts[h], NEG_BIG) if masked else sts[h]
            m_new = jnp.maximum(carry[h][0], jnp.max(st, axis=0, keepdims=True))
            soft.append((m_new, jnp.exp2(carry[h][0] - m_new), jnp.exp2(st - m_new).astype(bf16)))
        new = []
        for h in heads:
            m_new, alpha, p = soft[h]
            vtj = vt_ref[h * VT_ROWS:(h + 1) * VT_ROWS, pl.ds(start, TK)]
            new.append((m_new, alpha * carry[h][1] + _dot(vtj, p)))
        return tuple(new)

    one = (jnp.full((1, tq), NEG_BIG, f32), jnp.zeros((VT_ROWS, tq), f32))
    carry = lax.fori_loop(0, i, functools.partial(step, masked=False), (one,) * ATT_HEADS)
    carry = step(i, carry, True)
    out_t = jnp.concatenate([acc[:FOX_HEAD_DIM] / acc[FOX_HEAD_DIM:FOX_HEAD_DIM + 1] for _, acc in carry], axis=0)
    o_ref[0] = jnp.transpose(out_t).astype(bf16)


def _fox_attn(qt_aug, k_aug, vt, bsz, s):
    assert TQ == TK
    nq = s // TQ
    return pl.pallas_call(
        _fox_attn_kernel,
        out_shape=jax.ShapeDtypeStruct((bsz, s, FOX_WIDTH), bf16),
        grid=(bsz, FOX_HEADS // ATT_HEADS, nq),
        in_specs=[pl.BlockSpec((ATT_HEADS * AUG_LANES, TQ), lambda bi, hp, i: (hp, bi * nq + i)),
                  pl.BlockSpec((1, s, ATT_HEADS * AUG_LANES), lambda bi, hp, i: (bi, 0, hp)),
                  pl.BlockSpec((ATT_HEADS * VT_ROWS, s), lambda bi, hp, i: (hp, bi))],
        out_specs=pl.BlockSpec((1, TQ, ATT_HEADS * FOX_HEAD_DIM), lambda bi, hp, i: (bi, i, hp)),
        compiler_params=_params(("arbitrary", "arbitrary", "arbitrary")),
        name="fox_attn",
    )(qt_aug, k_aug, vt)


def _post_mix_kernel(gu_ref, vn_ref, sa_ref, sb_ref, yb_ref, xn_ref, km_ref, vm_ref,
                     wsp_ref, bsp_ref, wa_ref, wb_ref, wo_ref, g1_ref, b1_ref,
                     wmq_ref, wmo_ref, g2_ref, b2_ref, wr_ref, br_ref,
                     x2_ref, x2b_ref, ri_ref, rw_ref, cnt_ref):
    tm = gu_ref.shape[0]
    subs = [slice(s0, s0 + POST_SUB) for s0 in range(0, tm, POST_SUB)]
    each = lambda fn, *lists: [fn(*args) for args in zip(*lists)] if lists else [fn(sl) for sl in subs]
    lane = lax.broadcasted_iota(jnp.int32, (GM_BLOCK, LANES), 1)

    def spatial(sl):
        blocks = []
        for r0 in range(sl.start, sl.stop, GM_BLOCK):
            pairs = []
            for p in range(GM_GROUPS // 2):
                vp = vn_ref[r0:r0 + GM_BLOCK, p * LANES:(p + 1) * LANES]
                lo = _dot(wsp_ref[2 * p], vp)
                hi = _dot(wsp_ref[2 * p + 1], vp)
                pairs.append(jnp.where(lane < GM_WIDTH // GM_GROUPS, lo, hi))
            blocks.append(jnp.concatenate(pairs, axis=1) + bsp_ref[...])
        return (gu_ref[sl, :].astype(f32) * jnp.concatenate(blocks, axis=0)).astype(bf16)

    y_a = each(spatial)
    br_a = each(lambda y: _dot(y, wa_ref[...]), y_a)
    br_b = each(lambda sl: _dot(yb_ref[sl, :], wb_ref[...]))
    merged = each(lambda sl, a, b: (sa_ref[sl, :].astype(f32) * a + sb_ref[sl, :].astype(f32) * b).astype(bf16),
                  subs, br_a, br_b)
    proj = each(lambda m: _dot(m, wo_ref[...]), merged)
    x1 = each(lambda sl, p: _ln(DEEPNORM_ALPHA * xn_ref[sl, :] + p, g1_ref[...], b1_ref[...]), subs, proj)
    qm = each(lambda v: (_dot(v.astype(bf16), wmq_ref[...]) * (MEM_HEAD_DIM ** -0.5)).astype(bf16), x1)
    heads = [[] for _ in subs]
    for h in range(MEM_HEADS):
        sl_h = slice(h * MEM_HEAD_DIM, (h + 1) * MEM_HEAD_DIM)
        sc = each(lambda q: _dot_nt(q[:, sl_h], km_ref[:, sl_h]), qm)
        pr = each(lambda s: jnp.exp(s - jnp.max(s, axis=-1, keepdims=True)), sc)
        pr = each(lambda p: (p / jnp.sum(p, axis=-1, keepdims=True)).astype(bf16), pr)
        for hs, p in zip(heads, pr):
            hs.append(_dot(p, vm_ref[:, sl_h]))
    o = each(lambda hs: jnp.concatenate(hs, axis=1).astype(bf16), heads)
    proj2 = each(lambda v: _dot(v, wmo_ref[...]), o)
    x2 = each(lambda a, p: _ln(DEEPNORM_ALPHA * a + p, g2_ref[...], b2_ref[...]), x1, proj2)

    def route(sl, v):
        x2_ref[sl, :] = v
        x2b_ref[sl, :] = v.astype(bf16)
        v_hi = v.astype(bf16)
        v_lo = (v - v_hi.astype(f32)).astype(bf16)
        both = _dot(v_hi, wr_ref[...])
        logits = both[:, :LANES] + both[:, LANES:] + _dot(v_lo, wr_ref[:, :LANES]) + br_ref[...]
        n = v.shape[0]
        lane_t = lax.broadcasted_iota(jnp.int32, (n, LANES), 1)
        work = jnp.where(lane_t < N_EXPERTS, logits, -jnp.inf)
        tops, idxs = [], []
        for _ in range(TOP_K):
            mk = jnp.max(work, axis=-1, keepdims=True)
            ik = jnp.min(jnp.where(work == mk, lane_t, LANES), axis=-1, keepdims=True)
            work = jnp.where(lane_t == ik, -jnp.inf, work)
            tops.append(mk)
            idxs.append(ik)
        exps = [jnp.exp(tk - tops[0]) for tk in tops]
        denom = exps[0] + exps[1] + exps[2] + exps[3]
        member = jnp.zeros((n, LANES), f32)
        ri = jnp.zeros((n, LANES), jnp.int32)
        rw = jnp.zeros((n, LANES), f32)
        for k in range(TOP_K):
            member = member + jnp.where(lane_t == idxs[k], 1.0, 0.0)
            ri = jnp.where(lane_t == k, idxs[k], ri)
            rw = jnp.where(lane_t == k, exps[k] / denom, rw)
        ri_ref[sl, :] = ri
        rw_ref[sl, :] = rw
        return jnp.sum(member, axis=0, keepdims=True)

    counts = each(route, subs, x2)
    cnt_ref[0] = functools.reduce(lambda a, b: a + b, counts)


def _post_mix(gu, vn, sa, sb, yb, xn, k_mem, v_mem, wsp, bsp, wa, wb, wo, g1, b1, wmq, wmo, g2, b2, wr, br,
              seq, m_len):
    t = gu.shape[0]
    tm = TM_POST
    per_seq = seq // tm
    row = lambda w: pl.BlockSpec((tm, w), lambda i: (i, 0))
    memspec = pl.BlockSpec((m_len, D_MODEL), lambda i: (i // per_seq, 0))
    consts = [wsp, bsp, wa, wb, wo, g1, b1, wmq, wmo, g2, b2, wr, br]
    return pl.pallas_call(
        _post_mix_kernel,
        out_shape=(jax.ShapeDtypeStruct((t, D_MODEL), f32),
                   jax.ShapeDtypeStruct((t, D_MODEL), bf16),
                   jax.ShapeDtypeStruct((t, LANES), jnp.int32),
                   jax.ShapeDtypeStruct((t, LANES), f32),
                   jax.ShapeDtypeStruct((t // tm, 1, LANES), f32)),
        grid=(t // tm,),
        in_specs=[row(GM_WIDTH), row(GM_WIDTH), row(D_MODEL), row(D_MODEL), row(FOX_WIDTH), row(D_MODEL),
                  memspec, memspec] + [_const_spec(c.shape) for c in consts],
        out_specs=(row(D_MODEL), row(D_MODEL), row(LANES), row(LANES),
                   pl.BlockSpec((1, 1, LANES), lambda i: (i, 0, 0))),
        compiler_params=_params(("arbitrary",)),
        name="post_mix",
    )(gu, vn, sa, sb, yb, xn, k_mem, v_mem, *consts)


def _run_copies(n_s, lst_s, dst_s, first, n_runs, local_ref, global_ref, sem, to_global, op):
    def per_run(r, _):
        n = n_s[first + r]
        ls = 0 if lst_s is None else lst_s[first + r]
        gd = dst_s[first + r]
        size = MAX_RUN_UNITS
        while size >= 1:
            done = (n // (2 * size)) * (2 * size)

            @pl.when((n & size) != 0)
            def _(size=size, done=done):
                lo = 0 if lst_s is None else pl.multiple_of((ls + done) * ROW_ALIGN, ROW_ALIGN)
                go = pl.multiple_of((gd + done) * ROW_ALIGN, ROW_ALIGN)
                loc = local_ref.at[pl.ds(lo, size * ROW_ALIGN)]
                glo = global_ref.at[pl.ds(go, size * ROW_ALIGN)]
                cp = pltpu.make_async_copy(loc, glo, sem) if to_global else pltpu.make_async_copy(glo, loc, sem)
                cp.start() if op == "start" else cp.wait()

            size //= 2
        return 0

    lax.fori_loop(0, n_runs, per_run, 0)


def _piece_copies(npc_s, psrc_s, pdst_s, tile, local_ref, global_ref, sem, to_global, op):
    for c, size in enumerate(PIECE_SIZES):
        cls = tile * len(PIECE_SIZES) + c

        def one(k, _, size=size, cls=cls):
            lo = pl.multiple_of(psrc_s[cls * N_EXPERTS + k] * ROW_ALIGN, ROW_ALIGN)
            go = pl.multiple_of(pdst_s[cls * N_EXPERTS + k] * ROW_ALIGN, ROW_ALIGN)
            loc = local_ref.at[pl.ds(lo, size * ROW_ALIGN)]
            glo = global_ref.at[pl.ds(go, size * ROW_ALIGN)]
            cp = pltpu.make_async_copy(loc, glo, sem) if to_global else pltpu.make_async_copy(glo, loc, sem)
            cp.start() if op == "start" else cp.wait()
            return 0

        lax.fori_loop(0, npc_s[cls], one, 0)


def _dispatch_kernel(npc_s, psrc_s, pdst_s, zn_s, zdst_s, ri_ref, lstc_ref, x_ref, xs_ref, xg_ref, zero_ref, sem):
    i = pl.program_id(0)
    slot = i % 2
    rt = x_ref.shape[0]
    n_zero = zn_s.shape[0]
    idx_t = jnp.transpose(ri_ref[...].astype(f32))
    sub = lax.broadcasted_iota(jnp.int32, (LANES, rt), 0).astype(f32)
    member_t = jnp.zeros((LANES, rt), f32)
    for k in range(TOP_K):
        member_t = member_t + jnp.where(sub == idx_t[k:k + 1, :], 1.0, 0.0)
    r_ids = lax.broadcasted_iota(jnp.int32, (rt, rt), 0)
    c_ids = lax.broadcasted_iota(jnp.int32, (rt, rt), 1)
    earlier = jnp.where(r_ids < c_ids, 1.0, 0.0).astype(bf16)
    base = lstc_ref[0] + _dot(member_t.astype(bf16), earlier)
    row_i = lax.broadcasted_iota(jnp.int32, (RUN_ROWS, rt), 0)
    sel = jnp.zeros((RUN_ROWS, rt), f32)
    for k in range(TOP_K):
        lpos = jnp.sum(jnp.where(sub == idx_t[k:k + 1, :], base, 0.0), axis=0, keepdims=True)
        sel = jnp.where(row_i == lpos.astype(jnp.int32), 1.0, sel)
    sel = sel.astype(bf16)

    half = D_MODEL // 2
    xg_ref[slot, :, :half] = _dot(sel, x_ref[:, :half]).astype(bf16)
    xg_ref[slot, :, half:] = _dot(sel, x_ref[:, half:]).astype(bf16)
    _piece_copies(npc_s, psrc_s, pdst_s, i, xg_ref.at[slot], xs_ref, sem.at[slot], True, "start")

    @pl.when(i > 0)
    def _():
        _piece_copies(npc_s, psrc_s, pdst_s, i - 1, xg_ref.at[1 - slot], xs_ref, sem.at[1 - slot], True, "wait")

    @pl.when(i == pl.num_programs(0) - 1)
    def _():
        zero_ref[...] = jnp.zeros_like(zero_ref)
        _run_copies(zn_s, None, zdst_s, 0, n_zero, zero_ref, xs_ref, sem.at[2], True, "start")
        _run_copies(zn_s, None, zdst_s, 0, n_zero, zero_ref, xs_ref, sem.at[2], True, "wait")
        _piece_copies(npc_s, psrc_s, pdst_s, i, xg_ref.at[slot], xs_ref, sem.at[slot], True, "wait")


def _dispatch(npc_s, psrc_s, pdst_s, zn_s, zdst_s, ri, lst_col, x2b, n_rows):
    t = x2b.shape[0]
    grid_spec = pltpu.PrefetchScalarGridSpec(
        num_scalar_prefetch=5,
        grid=(t // RT,),
        in_specs=[pl.BlockSpec((RT, LANES), lambda i, *_: (i, 0)),
                  pl.BlockSpec((1, LANES, 1), lambda i, *_: (i, 0, 0)),
                  pl.BlockSpec((RT, D_MODEL), lambda i, *_: (i, 0))],
        out_specs=pl.BlockSpec(memory_space=pl.ANY),
        scratch_shapes=[pltpu.VMEM((2, RUN_ROWS, D_MODEL), bf16), pltpu.VMEM((RT, D_MODEL), bf16),
                        pltpu.SemaphoreType.DMA((3,))],
    )
    return pl.pallas_call(
        _dispatch_kernel,
        out_shape=jax.ShapeDtypeStruct((n_rows, D_MODEL), bf16),
        grid_spec=grid_spec,
        compiler_params=_params(("arbitrary",)),
        name="dispatch",
    )(npc_s, psrc_s, pdst_s, zn_s, zdst_s, ri, lst_col, x2b)


def _experts_kernel(et_s, newexp_s, nu_s, xs_ref, wgu_ref, bgu_ref, wd_ref, bd_ref, ys_ref, wgu_bf, wd_bf):
    w = pl.program_id(0)

    @pl.when(w < nu_s[0])
    def _():
        @pl.when(newexp_s[w] == 1)
        def _():
            wgu_bf[...] = wgu_ref[0].astype(bf16)
            wd_bf[...] = wd_ref[0].astype(bf16)

        h = _dot(xs_ref[...], wgu_bf[...]) + bgu_ref[0]
        gate = jnp.minimum(h[:, :D_EXPERT], SWIGLU_LIMIT)
        up = jnp.clip(h[:, D_EXPERT:], -SWIGLU_LIMIT, SWIGLU_LIMIT)
        glu = gate / (1.0 + jnp.exp2(gate * (-SWIGLU_ALPHA * LOG2E)))
        ys_ref[...] = (_dot(((up + 1.0) * glu).astype(bf16), wd_bf[...]) + bd_ref[0]).astype(bf16)

    @pl.when(w >= nu_s[0])
    def _():
        ys_ref[...] = jnp.zeros_like(ys_ref)


def _experts(et, newexp, n_used, xs, wgu, bgu, wd, bd):
    n_rows = xs.shape[0]
    tm = TM_EXP
    used = lambda w, nu: jnp.minimum(w, nu[0] - 1)
    wsel = lambda w, et, newexp, nu: (et[used(w, nu)], 0, 0)
    grid_spec = pltpu.PrefetchScalarGridSpec(
        num_scalar_prefetch=3,
        grid=(n_rows // tm,),
        in_specs=[pl.BlockSpec((tm, D_MODEL), lambda w, et, newexp, nu: (used(w, nu), 0)),
                  pl.BlockSpec((1, D_MODEL, 2 * D_EXPERT), wsel),
                  pl.BlockSpec((1, 1, 2 * D_EXPERT), wsel),
                  pl.BlockSpec((1, D_EXPERT, D_MODEL), wsel),
                  pl.BlockSpec((1, 1, D_MODEL), wsel)],
        out_specs=pl.BlockSpec((tm, D_MODEL), lambda w, et, newexp, nu: (w, 0)),
        scratch_shapes=[pltpu.VMEM((D_MODEL, 2 * D_EXPERT), bf16), pltpu.VMEM((D_EXPERT, D_MODEL), bf16)],
    )
    return pl.pallas_call(
        _experts_kernel,
        out_shape=jax.ShapeDtypeStruct((n_rows, D_MODEL), bf16),
        grid_spec=grid_spec,
        compiler_params=_params(("arbitrary",)),
        name="experts",
    )(et, newexp, n_used, xs, wgu, bgu, wd, bd)


def _combine_kernel(npc_s, psrc_s, pdst_s, ri_ref, rw_ref, lstr_ref, x2_ref, g_ref, b_ref, ys_ref, o_ref,
                    yg_ref, sem):
    i = pl.program_id(0)
    slot = i % 2
    rt = x2_ref.shape[0]

    @pl.when(i == 0)
    def _():
        yg_ref[...] = jnp.zeros_like(yg_ref)
        _piece_copies(npc_s, psrc_s, pdst_s, i, yg_ref.at[slot], ys_ref, sem.at[slot], False, "start")

    @pl.when(i + 1 < pl.num_programs(0))
    def _():
        _piece_copies(npc_s, psrc_s, pdst_s, i + 1, yg_ref.at[1 - slot], ys_ref, sem.at[1 - slot], False, "start")

    _piece_copies(npc_s, psrc_s, pdst_s, i, yg_ref.at[slot], ys_ref, sem.at[slot], False, "wait")

    idx = ri_ref[...]
    rw = rw_ref[...]
    lane = lax.broadcasted_iota(jnp.int32, (rt, LANES), 1)
    member = jnp.zeros((rt, LANES), f32)
    for k in range(TOP_K):
        member = member + jnp.where(lane == idx[:, k:k + 1], 1.0, 0.0)
    r_ids = lax.broadcasted_iota(jnp.int32, (rt, rt), 0)
    c_ids = lax.broadcasted_iota(jnp.int32, (rt, rt), 1)
    earlier = jnp.where(r_ids > c_ids, 1.0, 0.0).astype(bf16)
    base = lstr_ref[0] + _dot(earlier, member.astype(bf16))
    lane_w = lax.broadcasted_iota(jnp.int32, (rt, RUN_ROWS), 1)
    sel = jnp.zeros((rt, RUN_ROWS), f32)
    for k in range(TOP_K):
        lpos = jnp.sum(jnp.where(lane == idx[:, k:k + 1], base, 0.0), axis=-1, keepdims=True)
        sel = jnp.where(lane_w == lpos.astype(jnp.int32), rw[:, k:k + 1], sel)
    sel = sel.astype(bf16)

    y = _dot(sel, yg_ref[slot])
    o_ref[...] = _ln(DEEPNORM_ALPHA * x2_ref[...] + y, g_ref[...], b_ref[...])


def _combine(npc_s, psrc_s, pdst_s, ri, rw, lst_row, x2, g3, b3, ys):
    t = x2.shape[0]
    row = lambda w: pl.BlockSpec((RT, w), lambda i, *_: (i, 0))
    const = lambda a: pl.BlockSpec(a.shape, lambda i, *_: (0,) * a.ndim)
    grid_spec = pltpu.PrefetchScalarGridSpec(
        num_scalar_prefetch=3,
        grid=(t // RT,),
        in_specs=[row(LANES), row(LANES), pl.BlockSpec((1, 1, LANES), lambda i, *_: (i, 0, 0)),
                  row(D_MODEL), const(g3), const(b3), pl.BlockSpec(memory_space=pl.ANY)],
        out_specs=row(D_MODEL),
        scratch_shapes=[pltpu.VMEM((2, RUN_ROWS, D_MODEL), bf16), pltpu.SemaphoreType.DMA((2,))],
    )
    return pl.pallas_call(
        _combine_kernel,
        out_shape=jax.ShapeDtypeStruct((t, D_MODEL), f32),
        grid_spec=grid_spec,
        compiler_params=_params(("arbitrary",)),
        name="combine",
    )(npc_s, psrc_s, pdst_s, ri, rw, lst_row, x2, g3, b3, ys)


def _augment_qk(w_qk, scale):
    d = w_qk.shape[0]
    w_h = (w_qk * scale).reshape(d, FOX_HEADS, FOX_HEAD_DIM)
    return jnp.pad(w_h, ((0, 0), (0, 0), (0, AUG_LANES - FOX_HEAD_DIM))).reshape(d, FOX_HEADS * AUG_LANES)


def _bias_lane_tables():
    src = jnp.arange(LANES)[:, None]
    dst = jnp.arange(FOX_HEADS * AUG_LANES)[None, :]
    head, off = dst // AUG_LANES, dst % AUG_LANES - FOX_HEAD_DIM
    in_q = (off >= 0) & (off < N_SPLIT)
    in_k = (off >= N_SPLIT) & (off < 2 * N_SPLIT)
    eq = jnp.where(in_q & (src == N_SPLIT * head + off), 1.0, 0.0)
    ek = jnp.where(in_k & (src == N_SPLIT * head + off - N_SPLIT), -1.0, 0.0)
    return eq.astype(bf16), ek.astype(bf16), in_k.astype(f32), in_q.astype(f32)


def _round_up(a, m):
    return (a + m - 1) // m * m


def kernel(x, mem, ln_in_g, ln_in_b, w_in, b_forget, ln_v_g, ln_v_b, w_spatial, b_spatial, w_branch_a, w_branch_b, w_out, ln1_g, ln1_b, ln_mem_g, ln_mem_b, w_mq, w_mkv, w_mo, ln2_g, ln2_b, w_router, b_router, w_gate_up, b_gate_up, w_down, b_down, ln3_g, ln3_b):
    bsz, seq, d = x.shape
    m_len = mem.shape[1]
    t = bsz * seq
    assert d == D_MODEL and w_in.shape[0] == DEPTH == 1
    assert seq % TQ == 0 and seq % TM_IN == 0 and seq % TM_POST == 0
    assert t % RT == 0 and RT % TM_POST == 0 and RT == TM_EXP
    vec = lambda a: a.reshape(1, -1).astype(f32)
    l = 0
    o = [0]
    for w in (GM_WIDTH, GM_WIDTH, FOX_WIDTH, FOX_WIDTH, FOX_WIDTH, FOX_HEADS, D_MODEL, D_MODEL):
        o.append(o[-1] + w)
    wi = w_in[l]
    w_f = wi[:, o[5]:o[6]]
    wuv = wi[:, o[0]:o[2]].astype(bf16)
    wq = _augment_qk(wi[:, o[2]:o[3]], FOX_HEAD_DIM ** -0.5 * LOG2E).astype(bf16)
    wk = _augment_qk(wi[:, o[3]:o[4]], 1.0).astype(bf16)
    wv_h = wi[:, o[4]:o[5]].T.reshape(FOX_HEADS, FOX_HEAD_DIM, d)
    wvt = jnp.pad(wv_h, ((0, 0), (0, VT_ROWS - FOX_HEAD_DIM), (0, 0))).reshape(FOX_HEADS * VT_ROWS, d).astype(bf16)
    vone = jnp.tile((jnp.arange(VT_ROWS) >= FOX_HEAD_DIM).astype(f32), FOX_HEADS).reshape(-1, 1)
    wg = wi[:, o[6]:o[8]].astype(bf16)
    rep3 = lambda a: jnp.pad(jnp.repeat(a, N_SPLIT, axis=-1), ((0, 0), (0, LANES - N_SPLIT * FOX_HEADS)))
    wf = rep3(w_f).astype(bf16)
    bf3 = rep3(b_forget[l][None, :].astype(f32))
    eq, ek, cq, ck = _bias_lane_tables()
    pos = jnp.arange(GM_BLOCK)
    chunk_mask = (pos[None, :] // CHUNK) <= (pos[:, None] // CHUNK)
    wsp = jnp.where(chunk_mask[None], w_spatial[l], 0).astype(bf16)
    bsp = jnp.repeat(b_spatial[l].T, GM_WIDTH // GM_GROUPS, axis=1).astype(f32)
    wr_f = jnp.pad(w_router[l], ((0, 0), (0, LANES - N_EXPERTS))).astype(f32)
    wr_hi = wr_f.astype(bf16)
    wr = jnp.concatenate([wr_hi, (wr_f - wr_hi.astype(f32)).astype(bf16)], axis=1)
    br = jnp.pad(b_router[l], (0, LANES - N_EXPERTS)).reshape(1, LANES).astype(f32)

    k_mem, v_mem = _mem_kv(mem.reshape(bsz * m_len, d), vec(ln_mem_g[l]), vec(ln_mem_b[l]),
                           w_mkv[l].astype(bf16), m_len)
    xn, gu, vn, qt_aug, k_aug, vt, sa, sb = _in_proj(
        x.reshape(t, d), vec(ln_in_g), vec(ln_in_b), wuv, wq, wk, wvt, vone, wg, wf, bf3, eq, ek, cq, ck,
        vec(ln_v_g[l]), vec(ln_v_b[l]), seq)
    yb = _fox_attn(qt_aug, k_aug.reshape(bsz, seq, -1), vt, bsz, seq)
    x2, x2b, ri, rw, cnt_tiles = _post_mix(
        gu, vn, sa, sb, yb.reshape(t, FOX_WIDTH), xn, k_mem, v_mem, wsp, bsp,
        w_branch_a[l].astype(bf16), w_branch_b[l].astype(bf16), w_out[l].astype(bf16),
        vec(ln1_g[l]), vec(ln1_b[l]), w_mq[l].astype(bf16), w_mo[l].astype(bf16),
        vec(ln2_g[l]), vec(ln2_b[l]), wr, br, seq, m_len)

    i32 = jnp.int32
    n_rt = t // RT
    cnt = cnt_tiles[:, 0, :N_EXPERTS].astype(i32).reshape(n_rt, RT // TM_POST, N_EXPERTS).sum(axis=1)
    cnt8 = _round_up(cnt, ROW_ALIGN)
    lst = jnp.cumsum(cnt8, axis=1) - cnt8
    rows_e = cnt8.sum(axis=0)
    region = _round_up(rows_e, TM_EXP)
    e_end = jnp.cumsum(region)
    e_off = e_end - region
    dst = e_off[None, :] + jnp.cumsum(cnt8, axis=0) - cnt8
    n_tiles = -(-(t * TOP_K + n_rt * N_EXPERTS * (ROW_ALIGN - 1)) // TM_EXP) + N_EXPERTS
    n_used = (e_end[-1:] // TM_EXP).astype(i32)
    tile_ids = jnp.arange(n_tiles, dtype=i32)
    e_t = jnp.minimum(jnp.sum(tile_ids[:, None] * TM_EXP >= e_end[None, :], axis=1), N_EXPERTS - 1).astype(i32)
    newexp = jnp.concatenate([jnp.ones((1,), i32), (e_t[1:] != e_t[:-1]).astype(i32)])
    n_tail = n_tiles - t * TOP_K // TM_EXP
    tail_tile = n_used[0] + jnp.arange(n_tail, dtype=i32)
    z_dst = jnp.concatenate([e_off + rows_e, tail_tile * TM_EXP])
    z_n = jnp.concatenate([region - rows_e, jnp.where(tail_tile < n_tiles, TM_EXP, 0)])
    octs = lambda a: (a.reshape(-1) // ROW_ALIGN).astype(i32)
    lst_pad = jnp.pad(lst, ((0, 0), (0, LANES - N_EXPERTS))).astype(f32)
    n_u, lst_u, dst_u = cnt8 // ROW_ALIGN, lst // ROW_ALIGN, dst // ROW_ALIGN
    sizes = jnp.asarray(PIECE_SIZES, i32)[None, :, None]
    has = ((n_u[:, None, :] & sizes) != 0).astype(i32)
    done_u = n_u[:, None, :] // (2 * sizes) * (2 * sizes)
    slot_of = jnp.cumsum(has, axis=-1) - 1
    put = has[..., None, :] * (slot_of[..., None, :] == jnp.arange(N_EXPERTS, dtype=i32)[:, None])
    listed = lambda a: jnp.sum(put * (a[:, None, :] + done_u)[..., None, :], axis=-1).reshape(-1).astype(i32)
    npc = has.sum(axis=-1).reshape(-1).astype(i32)
    psrc, pdst = listed(lst_u), listed(dst_u)

    xs = _dispatch(npc, psrc, pdst, octs(z_n), octs(z_dst), ri, lst_pad[:, :, None], x2b, n_tiles * TM_EXP)
    ys = _experts(e_t, newexp, n_used, xs, w_gate_up[l], b_gate_up[l][:, None, :].astype(f32),
                  w_down[l], b_down[l][:, None, :].astype(f32))
    out = _combine(npc, psrc, pdst, ri, rw, lst_pad[:, None, :], x2, vec(ln3_g[l]), vec(ln3_b[l]), ys)
    return out.reshape(bsz, seq, d)
```

```python
import functools
import math

import jax
import jax.numpy as jnp
from jax import lax
from jax.experimental import pallas as pl
from jax.experimental.pallas import tpu as pltpu

D_MODEL = 1024
CHUNK = 64
GM_WIDTH = 512
GM_GROUPS = 8
GM_BLOCK = 128
FOX_HEADS = 8
FOX_HEAD_DIM = 64
FOX_WIDTH = FOX_HEADS * FOX_HEAD_DIM
MEM_HEADS = 4
MEM_HEAD_DIM = D_MODEL // MEM_HEADS
N_EXPERTS = 32
TOP_K = 4
D_EXPERT = D_MODEL
SWIGLU_ALPHA = 1.702
SWIGLU_LIMIT = 7.0
LN_EPS = 1e-5
DEPTH = 1
DEEPNORM_ALPHA = (2 * DEPTH) ** 0.25

LANES = 128
ROW_ALIGN = 16
AUG_LANES = 2 * FOX_HEAD_DIM
N_SPLIT = 3
VMEM_LIMIT = 56 * 1024 * 1024
NEG_BIG = -1e30
LOG2E = math.log2(math.e)

TM_IN = 512
TQ = 512
TK = 512
ATT_HEADS = 4
VT_ROWS = FOX_HEAD_DIM + 16
TM_POST = 512
POST_SUB = 256
RT = 512
TM_EXP = 512
RUN_ROWS = TOP_K * RT + N_EXPERTS * ROW_ALIGN
MAX_RUN_UNITS = RT // ROW_ALIGN
SEL_BLOCKS = 2
PIECE_SIZES = tuple(MAX_RUN_UNITS >> b for b in range(MAX_RUN_UNITS.bit_length()))

bf16 = jnp.bfloat16
f32 = jnp.float32


def _ln(x, g, b):
    mu = jnp.mean(x, axis=-1, keepdims=True)
    xc = x - mu
    var = jnp.mean(xc * xc, axis=-1, keepdims=True)
    return xc * lax.rsqrt(var + LN_EPS) * g + b


def _dot(a, b):
    return jnp.dot(a, b, preferred_element_type=f32)


def _dot_nt(a, b):
    return lax.dot_general(a, b, (((1,), (1,)), ((), ())), preferred_element_type=f32)


def _params(sem):
    return pltpu.CompilerParams(dimension_semantics=sem, vmem_limit_bytes=VMEM_LIMIT)


def _const_spec(shape):
    nd = len(shape)
    return pl.BlockSpec(shape, lambda *_: (0,) * nd, pipeline_mode=pl.Buffered(1))


def _mem_kv_kernel(mem_ref, g_ref, b_ref, w_ref, k_ref, v_ref):
    mn = _ln(mem_ref[...], g_ref[...], b_ref[...]).astype(bf16)
    kv = _dot(mn, w_ref[...])
    k_ref[...] = kv[:, :D_MODEL].astype(bf16)
    v_ref[...] = kv[:, D_MODEL:].astype(bf16)


def _mem_kv(mem2, g, b, w_mkv, m_len):
    rows = mem2.shape[0]
    return pl.pallas_call(
        _mem_kv_kernel,
        out_shape=(jax.ShapeDtypeStruct((rows, D_MODEL), bf16),) * 2,
        grid=(rows // m_len,),
        in_specs=[pl.BlockSpec((m_len, D_MODEL), lambda i: (i, 0)),
                  _const_spec((1, D_MODEL)), _const_spec((1, D_MODEL)),
                  _const_spec((D_MODEL, 2 * D_MODEL))],
        out_specs=(pl.BlockSpec((m_len, D_MODEL), lambda i: (i, 0)),) * 2,
        compiler_params=_params(("arbitrary",)),
        name="mem_kv",
    )(mem2, g, b, w_mkv)


def _in_proj_kernel(tiles_per_seq, x_ref, g_ref, b_ref, wuv_ref, wq_ref, wk_ref, wvt_ref, vone_ref, wg_ref,
                    wf_ref, bf_ref, eq_ref, ek_ref, cq_ref, ck_ref, lvg_ref, lvb_ref,
                    xn_ref, gu_ref, vn_ref, qt_ref, k_ref, vt_ref, sa_ref, sb_ref, carry_ref):
    tm = x_ref.shape[0]
    xn = _ln(x_ref[...], g_ref[...], b_ref[...])
    xn_ref[...] = xn
    xb = xn.astype(bf16)

    huv = _dot(xb, wuv_ref[...])
    gel = 0.5 * huv * (1.0 + lax.erf(huv * (1.0 / math.sqrt(2.0))))
    gu_ref[...] = gel[:, :GM_WIDTH].astype(bf16)
    vn_ref[...] = _ln(gel[:, GM_WIDTH:], lvg_ref[...], lvb_ref[...]).astype(bf16)

    vt_ref[...] = (_dot_nt(wvt_ref[...], xb) + vone_ref[...]).astype(bf16)

    hg = _dot(xb, wg_ref[...])
    sg = jax.nn.sigmoid(hg)
    sa_ref[...] = sg[:, :D_MODEL].astype(bf16)
    sb_ref[...] = sg[:, D_MODEL:].astype(bf16)

    z = _dot(xb, wf_ref[...]) + bf_ref[...]
    lane = lax.broadcasted_iota(jnp.int32, z.shape, 1)
    row = lax.broadcasted_iota(jnp.int32, z.shape, 0)
    log_f = jnp.minimum(z, 0.0) - jnp.log1p(jnp.exp(-jnp.abs(z)))
    c = jnp.where(lane < N_SPLIT * FOX_HEADS, log_f * LOG2E, 0.0)
    shift = 1
    while shift < tm:
        c = c + jnp.where(row >= shift, pltpu.roll(c, shift, 0), 0.0)
        shift *= 2

    @pl.when(pl.program_id(0) % tiles_per_seq == 0)
    def _():
        carry_ref[...] = jnp.zeros_like(carry_ref)

    c = c + carry_ref[...]
    carry_ref[...] = c[tm - 1:tm, :]
    hi = c.astype(bf16).astype(f32)
    r1 = c - hi
    mid = r1.astype(bf16).astype(f32)
    lo = r1 - mid
    piece = jnp.where(lane % N_SPLIT == 0, hi, jnp.where(lane % N_SPLIT == 1, mid, lo)).astype(bf16)
    q_aug = _dot(xb, wq_ref[...]) + _dot(piece, eq_ref[...]) + cq_ref[...]
    qt_ref[...] = jnp.transpose(q_aug).astype(bf16)
    k_ref[...] = (_dot(xb, wk_ref[...]) + _dot(piece, ek_ref[...]) + ck_ref[...]).astype(bf16)


def _in_proj(x2, g, b, wuv, wq, wk, wvt, vone, wg, wf, bf3, eq, ek, cq, ck, lvg, lvb, seq):
    t = x2.shape[0]
    tm = TM_IN
    row = lambda w, dt: (jax.ShapeDtypeStruct((t, w), dt), pl.BlockSpec((tm, w), lambda i: (i, 0)))
    col = lambda h, dt: (jax.ShapeDtypeStruct((h, t), dt), pl.BlockSpec((h, tm), lambda i: (0, i)))
    outs = [row(D_MODEL, f32), row(GM_WIDTH, bf16), row(GM_WIDTH, bf16), col(FOX_HEADS * AUG_LANES, bf16),
            row(FOX_HEADS * AUG_LANES, bf16), col(wvt.shape[0], bf16), row(D_MODEL, bf16), row(D_MODEL, bf16)]
    consts = [g, b, wuv, wq, wk, wvt, vone, wg, wf, bf3, eq, ek, cq, ck, lvg, lvb]
    return pl.pallas_call(
        functools.partial(_in_proj_kernel, seq // tm),
        out_shape=tuple(o[0] for o in outs),
        grid=(t // tm,),
        in_specs=[pl.BlockSpec((tm, D_MODEL), lambda i: (i, 0))] + [_const_spec(c.shape) for c in consts],
        out_specs=tuple(o[1] for o in outs),
        scratch_shapes=[pltpu.VMEM((1, LANES), f32)],
        compiler_params=_params(("arbitrary",)),
        name="in_proj",
    )(x2, *consts)


def _fox_attn_kernel(qt_ref, k_ref, vt_ref, o_ref):
    i = pl.program_id(2)
    tq = qt_ref.shape[1]
    key_ids = lax.broadcasted_iota(jnp.int32, (TK, tq), 0)
    qry_ids = lax.broadcasted_iota(jnp.int32, (TK, tq), 1)
    heads = range(ATT_HEADS)
    qts = [qt_ref[h * AUG_LANES:(h + 1) * AUG_LANES, :] for h in heads]

    def step(j, carry, masked):
        start = pl.multiple_of(j * TK, TK)
        sts = [_dot(k_ref[0, pl.ds(start, TK), h * AUG_LANES:(h + 1) * AUG_LANES], qts[h]) for h in heads]
        soft = []
        for h in heads:
            st = jnp.where(key_ids <= qry_ids, sts[h], NEG_BIG) if masked else sts[h]
            m_new = jnp.maximum(carry[h][0], jnp.max(st, axis=0, keepdims=True))
            soft.append((m_new, jnp.exp2(carry[h][0] - m_new), jnp.exp2(st - m_new).astype(bf16)))
        new = []
        for h in heads:
            m_new, alpha, p = soft[h]
            vtj = vt_ref[h * VT_ROWS:(h + 1) * VT_ROWS, pl.ds(start, TK)]
            new.append((m_new, alpha * carry[h][1] + _dot(vtj, p)))
        return tuple(new)

    one = (jnp.full((1, tq), NEG_BIG, f32), jnp.zeros((VT_ROWS, tq), f32))
    carry = lax.fori_loop(0, i, functools.partial(step, masked=False), (one,) * ATT_HEADS)
    carry = step(i, carry, True)
    out_t = jnp.concatenate([acc[:FOX_HEAD_DIM] / acc[FOX_HEAD_DIM:FOX_HEAD_DIM + 1] for _, acc in carry], axis=0)
    o_ref[0] = jnp.transpose(out_t).astype(bf16)


def _fox_attn(qt_aug, k_aug, vt, bsz, s):
    assert TQ == TK
    nq = s // TQ
    return pl.pallas_call(
        _fox_attn_kernel,
        out_shape=jax.ShapeDtypeStruct((bsz, s, FOX_WIDTH), bf16),
        grid=(bsz, FOX_HEADS // ATT_HEADS, nq),
        in_specs=[pl.BlockSpec((ATT_HEADS * AUG_LANES, TQ), lambda bi, hp, i: (hp, bi * nq + i)),
                  pl.BlockSpec((1, s, ATT_HEADS * AUG_LANES), lambda bi, hp, i: (bi, 0, hp)),
                  pl.BlockSpec((ATT_HEADS * VT_ROWS, s), lambda bi, hp, i: (hp, bi))],
        out_specs=pl.BlockSpec((1, TQ, ATT_HEADS * FOX_HEAD_DIM), lambda bi, hp, i: (bi, i, hp)),
        compiler_params=_params(("arbitrary", "arbitrary", "arbitrary")),
        name="fox_attn",
    )(qt_aug, k_aug, vt)


def _post_mix_kernel(gu_ref, vn_ref, sa_ref, sb_ref, yb_ref, xn_ref, km_ref, vm_ref,
                     wsp_ref, bsp_ref, wa_ref, wb_ref, wo_ref, g1_ref, b1_ref,
                     wmq_ref, wmo_ref, g2_ref, b2_ref, wr_ref, br_ref,
                     x2_ref, x2b_ref, ri_ref, rw_ref, cnt_ref):
    tm = gu_ref.shape[0]
    subs = [slice(s0, s0 + POST_SUB) for s0 in range(0, tm, POST_SUB)]
    each = lambda fn, *lists: [fn(*args) for args in zip(*lists)] if lists else [fn(sl) for sl in subs]
    lane = lax.broadcasted_iota(jnp.int32, (GM_BLOCK, LANES), 1)

    def spatial(sl):
        blocks = []
        for r0 in range(sl.start, sl.stop, GM_BLOCK):
            pairs = []
            for p in range(GM_GROUPS // 2):
                vp = vn_ref[r0:r0 + GM_BLOCK, p * LANES:(p + 1) * LANES]
                lo = _dot(wsp_ref[2 * p], vp)
                hi = _dot(wsp_ref[2 * p + 1], vp)
                pairs.append(jnp.where(lane < GM_WIDTH // GM_GROUPS, lo, hi))
            blocks.append(jnp.concatenate(pairs, axis=1) + bsp_ref[...])
        return (gu_ref[sl, :].astype(f32) * jnp.concatenate(blocks, axis=0)).astype(bf16)

    y_a = each(spatial)
    br_a = each(lambda y: _dot(y, wa_ref[...]), y_a)
    br_b = each(lambda sl: _dot(yb_ref[sl, :], wb_ref[...]))
    merged = each(lambda sl, a, b: (sa_ref[sl, :].astype(f32) * a + sb_ref[sl, :].astype(f32) * b).astype(bf16),
                  subs, br_a, br_b)
    proj = each(lambda m: _dot(m, wo_ref[...]), merged)
    x1 = each(lambda sl, p: _ln(DEEPNORM_ALPHA * xn_ref[sl, :] + p, g1_ref[...], b1_ref[...]), subs, proj)
    qm = each(lambda v: (_dot(v.astype(bf16), wmq_ref[...]) * (MEM_HEAD_DIM ** -0.5)).astype(bf16), x1)
    heads = [[] for _ in subs]
    for h in range(MEM_HEADS):
        sl_h = slice(h * MEM_HEAD_DIM, (h + 1) * MEM_HEAD_DIM)
        sc = each(lambda q: _dot_nt(q[:, sl_h], km_ref[:, sl_h]), qm)
        pr = each(lambda s: jnp.exp(s - jnp.max(s, axis=-1, keepdims=True)), sc)
        pr = each(lambda p: (p / jnp.sum(p, axis=-1, keepdims=True)).astype(bf16), pr)
        for hs, p in zip(heads, pr):
            hs.append(_dot(p, vm_ref[:, sl_h]))
    o = each(lambda hs: jnp.concatenate(hs, axis=1).astype(bf16), heads)
    proj2 = each(lambda v: _dot(v, wmo_ref[...]), o)
    x2 = each(lambda a, p: _ln(DEEPNORM_ALPHA * a + p, g2_ref[...], b2_ref[...]), x1, proj2)

    def route(sl, v):
        x2_ref[sl, :] = v
        x2b_ref[sl, :] = v.astype(bf16)
        v_hi = v.astype(bf16)
        v_lo = (v - v_hi.astype(f32)).astype(bf16)
        both = _dot(v_hi, wr_ref[...])
        logits = both[:, :LANES] + both[:, LANES:] + _dot(v_lo, wr_ref[:, :LANES]) + br_ref[...]
        n = v.shape[0]
        lane_t = lax.broadcasted_iota(jnp.int32, (n, LANES), 1)
        work = jnp.where(lane_t < N_EXPERTS, logits, -jnp.inf)
        tops, idxs = [], []
        for _ in range(TOP_K):
            mk = jnp.max(work, axis=-1, keepdims=True)
            ik = jnp.min(jnp.where(work == mk, lane_t, LANES), axis=-1, keepdims=True)
            work = jnp.where(lane_t == ik, -jnp.inf, work)
            tops.append(mk)
            idxs.append(ik)
        exps = [jnp.exp(tk - tops[0]) for tk in tops]
        denom = exps[0] + exps[1] + exps[2] + exps[3]
        member = jnp.zeros((n, LANES), f32)
        ri = jnp.zeros((n, LANES), jnp.int32)
        rw = jnp.zeros((n, LANES), f32)
        for k in range(TOP_K):
            member = member + jnp.where(lane_t == idxs[k], 1.0, 0.0)
            ri = jnp.where(lane_t == k, idxs[k], ri)
            rw = jnp.where(lane_t == k, exps[k] / denom, rw)
        ri_ref[sl, :] = ri
        rw_ref[sl, :] = rw
        return jnp.sum(member, axis=0, keepdims=True)

    counts = each(route, subs, x2)
    cnt_ref[0] = functools.reduce(lambda a, b: a + b, counts)


def _post_mix(gu, vn, sa, sb, yb, xn, k_mem, v_mem, wsp, bsp, wa, wb, wo, g1, b1, wmq, wmo, g2, b2, wr, br,
              seq, m_len):
    t = gu.shape[0]
    tm = TM_POST
    per_seq = seq // tm
    row = lambda w: pl.BlockSpec((tm, w), lambda i: (i, 0))
    memspec = pl.BlockSpec((m_len, D_MODEL), lambda i: (i // per_seq, 0))
    consts = [wsp, bsp, wa, wb, wo, g1, b1, wmq, wmo, g2, b2, wr, br]
    return pl.pallas_call(
        _post_mix_kernel,
        out_shape=(jax.ShapeDtypeStruct((t, D_MODEL), f32),
                   jax.ShapeDtypeStruct((t, D_MODEL), bf16),
                   jax.ShapeDtypeStruct((t, LANES), jnp.int32),
                   jax.ShapeDtypeStruct((t, LANES), f32),
                   jax.ShapeDtypeStruct((t // tm, 1, LANES), f32)),
        grid=(t // tm,),
        in_specs=[row(GM_WIDTH), row(GM_WIDTH), row(D_MODEL), row(D_MODEL), row(FOX_WIDTH), row(D_MODEL),
                  memspec, memspec] + [_const_spec(c.shape) for c in consts],
        out_specs=(row(D_MODEL), row(D_MODEL), row(LANES), row(LANES),
                   pl.BlockSpec((1, 1, LANES), lambda i: (i, 0, 0))),
        compiler_params=_params(("arbitrary",)),
        name="post_mix",
    )(gu, vn, sa, sb, yb, xn, k_mem, v_mem, *consts)


def _zero_fill_copies(n_s, dst_s, zero_ref, global_ref, sem, op):
    def per_range(r, _):
        n = n_s[r]
        gd = dst_s[r]
        for size in PIECE_SIZES:
            done = (n // (2 * size)) * (2 * size)

            @pl.when((n & size) != 0)
            def _(size=size, done=done):
                go = pl.multiple_of((gd + done) * ROW_ALIGN, ROW_ALIGN)
                cp = pltpu.make_async_copy(zero_ref.at[pl.ds(0, size * ROW_ALIGN)],
                                           global_ref.at[pl.ds(go, size * ROW_ALIGN)], sem)
                cp.start() if op == "start" else cp.wait()

        return 0

    lax.fori_loop(0, n_s.shape[0], per_range, 0)


def _piece_copies(npc_s, psrc_s, pdst_s, tile, local_ref, global_ref, sem, to_global, op):
    for c, size in enumerate(PIECE_SIZES):
        cls = tile * len(PIECE_SIZES) + c

        def one(k, _, size=size, cls=cls):
            lo = pl.multiple_of(psrc_s[cls * N_EXPERTS + k] * ROW_ALIGN, ROW_ALIGN)
            go = pl.multiple_of(pdst_s[cls * N_EXPERTS + k] * ROW_ALIGN, ROW_ALIGN)
            loc = local_ref.at[pl.ds(lo, size * ROW_ALIGN)]
            glo = global_ref.at[pl.ds(go, size * ROW_ALIGN)]
            cp = pltpu.make_async_copy(loc, glo, sem) if to_global else pltpu.make_async_copy(glo, loc, sem)
            cp.start() if op == "start" else cp.wait()
            return 0

        lax.fori_loop(0, npc_s[cls], one, 0)


def _dispatch_kernel(npc_s, psrc_s, pdst_s, zn_s, zdst_s, ri_ref, lstc_ref, x_ref, xs_ref, xg_ref, zero_ref, sem):
    i = pl.program_id(0)
    slot = i % 2
    rt = x_ref.shape[0]
    idx_t = jnp.transpose(ri_ref[...].astype(f32))
    sub = lax.broadcasted_iota(jnp.int32, (LANES, rt), 0).astype(f32)
    member_t = jnp.zeros((LANES, rt), f32)
    for k in range(TOP_K):
        member_t = member_t + jnp.where(sub == idx_t[k:k + 1, :], 1.0, 0.0)
    r_ids = lax.broadcasted_iota(jnp.int32, (rt, rt), 0)
    c_ids = lax.broadcasted_iota(jnp.int32, (rt, rt), 1)
    earlier = jnp.where(r_ids < c_ids, 1.0, 0.0).astype(bf16)
    base = lstc_ref[0] + _dot(member_t.astype(bf16), earlier)
    lpos = [jnp.sum(jnp.where(sub == idx_t[k:k + 1, :], base, 0.0), axis=0, keepdims=True).astype(jnp.int32)
            for k in range(TOP_K)]
    blk = RUN_ROWS // SEL_BLOCKS
    row_i = lax.broadcasted_iota(jnp.int32, (blk, rt), 0)
    sels = []
    for r in range(SEL_BLOCKS):
        sel = jnp.zeros((blk, rt), f32)
        for k in range(TOP_K):
            sel = jnp.where(row_i == lpos[k] - r * blk, 1.0, sel)
        sels.append(sel.astype(bf16))
    for r, sel in enumerate(sels):
        xg_ref[slot, r * blk:(r + 1) * blk, :] = _dot(sel, x_ref[...]).astype(bf16)
    _piece_copies(npc_s, psrc_s, pdst_s, i, xg_ref.at[slot], xs_ref, sem.at[slot], True, "start")

    @pl.when(i > 0)
    def _():
        _piece_copies(npc_s, psrc_s, pdst_s, i - 1, xg_ref.at[1 - slot], xs_ref, sem.at[1 - slot], True, "wait")

    @pl.when(i == pl.num_programs(0) - 1)
    def _():
        zero_ref[...] = jnp.zeros_like(zero_ref)
        _zero_fill_copies(zn_s, zdst_s, zero_ref, xs_ref, sem.at[2], "start")
        _zero_fill_copies(zn_s, zdst_s, zero_ref, xs_ref, sem.at[2], "wait")
        _piece_copies(npc_s, psrc_s, pdst_s, i, xg_ref.at[slot], xs_ref, sem.at[slot], True, "wait")


def _dispatch(npc_s, psrc_s, pdst_s, zn_s, zdst_s, ri, lst_col, x2b, n_rows):
    t = x2b.shape[0]
    grid_spec = pltpu.PrefetchScalarGridSpec(
        num_scalar_prefetch=5,
        grid=(t // RT,),
        in_specs=[pl.BlockSpec((RT, LANES), lambda i, *_: (i, 0)),
                  pl.BlockSpec((1, LANES, 1), lambda i, *_: (i, 0, 0)),
                  pl.BlockSpec((RT, D_MODEL), lambda i, *_: (i, 0))],
        out_specs=pl.BlockSpec(memory_space=pl.ANY),
        scratch_shapes=[pltpu.VMEM((2, RUN_ROWS, D_MODEL), bf16), pltpu.VMEM((RT, D_MODEL), bf16),
                        pltpu.SemaphoreType.DMA((3,))],
    )
    return pl.pallas_call(
        _dispatch_kernel,
        out_shape=jax.ShapeDtypeStruct((n_rows, D_MODEL), bf16),
        grid_spec=grid_spec,
        compiler_params=_params(("arbitrary",)),
        name="dispatch",
    )(npc_s, psrc_s, pdst_s, zn_s, zdst_s, ri, lst_col, x2b)


def _experts_kernel(et_s, newexp_s, nu_s, xs_ref, wgu_ref, bgu_ref, wd_ref, bd_ref, ys_ref, wgu_bf, wd_bf):
    w = pl.program_id(0)

    @pl.when(w < nu_s[0])
    def _():
        @pl.when(newexp_s[w] == 1)
        def _():
            wgu_bf[...] = wgu_ref[0].astype(bf16)
            wd_bf[...] = wd_ref[0].astype(bf16)

        h = _dot(xs_ref[...], wgu_bf[...]) + bgu_ref[0]
        gate = jnp.minimum(h[:, :D_EXPERT], SWIGLU_LIMIT)
        up = jnp.clip(h[:, D_EXPERT:], -SWIGLU_LIMIT, SWIGLU_LIMIT)
        glu = gate / (1.0 + jnp.exp2(gate * (-SWIGLU_ALPHA * LOG2E)))
        ys_ref[...] = (_dot(((up + 1.0) * glu).astype(bf16), wd_bf[...]) + bd_ref[0]).astype(bf16)

    @pl.when(w >= nu_s[0])
    def _():
        ys_ref[...] = jnp.zeros_like(ys_ref)


def _experts(et, newexp, n_used, xs, wgu, bgu, wd, bd):
    n_rows = xs.shape[0]
    tm = TM_EXP
    used = lambda w, nu: jnp.minimum(w, nu[0] - 1)
    wsel = lambda w, et, newexp, nu: (et[used(w, nu)], 0, 0)
    grid_spec = pltpu.PrefetchScalarGridSpec(
        num_scalar_prefetch=3,
        grid=(n_rows // tm,),
        in_specs=[pl.BlockSpec((tm, D_MODEL), lambda w, et, newexp, nu: (used(w, nu), 0)),
                  pl.BlockSpec((1, D_MODEL, 2 * D_EXPERT), wsel),
                  pl.BlockSpec((1, 1, 2 * D_EXPERT), wsel),
                  pl.BlockSpec((1, D_EXPERT, D_MODEL), wsel),
                  pl.BlockSpec((1, 1, D_MODEL), wsel)],
        out_specs=pl.BlockSpec((tm, D_MODEL), lambda w, et, newexp, nu: (w, 0)),
        scratch_shapes=[pltpu.VMEM((D_MODEL, 2 * D_EXPERT), bf16), pltpu.VMEM((D_EXPERT, D_MODEL), bf16)],
    )
    return pl.pallas_call(
        _experts_kernel,
        out_shape=jax.ShapeDtypeStruct((n_rows, D_MODEL), bf16),
        grid_spec=grid_spec,
        compiler_params=_params(("arbitrary",)),
        name="experts",
    )(et, newexp, n_used, xs, wgu, bgu, wd, bd)


def _combine_kernel(npc_s, psrc_s, pdst_s, ri_ref, rw_ref, lstr_ref, x2_ref, g_ref, b_ref, ys_ref, o_ref,
                    yg_ref, sem):
    i = pl.program_id(0)
    slot = i % 2
    rt = x2_ref.shape[0]

    @pl.when(i == 0)
    def _():
        yg_ref[...] = jnp.zeros_like(yg_ref)
        _piece_copies(npc_s, psrc_s, pdst_s, i, yg_ref.at[slot], ys_ref, sem.at[slot], False, "start")

    @pl.when(i + 1 < pl.num_programs(0))
    def _():
        _piece_copies(npc_s, psrc_s, pdst_s, i + 1, yg_ref.at[1 - slot], ys_ref, sem.at[1 - slot], False, "start")

    _piece_copies(npc_s, psrc_s, pdst_s, i, yg_ref.at[slot], ys_ref, sem.at[slot], False, "wait")

    idx = ri_ref[...]
    rw = rw_ref[...]
    lane = lax.broadcasted_iota(jnp.int32, (rt, LANES), 1)
    member = jnp.zeros((rt, LANES), f32)
    for k in range(TOP_K):
        member = member + jnp.where(lane == idx[:, k:k + 1], 1.0, 0.0)
    r_ids = lax.broadcasted_iota(jnp.int32, (rt, rt), 0)
    c_ids = lax.broadcasted_iota(jnp.int32, (rt, rt), 1)
    earlier = jnp.where(r_ids > c_ids, 1.0, 0.0).astype(bf16)
    base = lstr_ref[0] + _dot(earlier, member.astype(bf16))
    lpos = [jnp.sum(jnp.where(lane == idx[:, k:k + 1], base, 0.0), axis=-1, keepdims=True).astype(jnp.int32)
            for k in range(TOP_K)]
    blk = rt // SEL_BLOCKS
    lane_w = lax.broadcasted_iota(jnp.int32, (blk, RUN_ROWS), 1)
    sels = []
    for r in range(SEL_BLOCKS):
        rows = slice(r * blk, (r + 1) * blk)
        sel = jnp.zeros((blk, RUN_ROWS), f32)
        for k in range(TOP_K):
            sel = jnp.where(lane_w == lpos[k][rows], rw[rows, k:k + 1], sel)
        sels.append(sel.astype(bf16))
    for r, sel in enumerate(sels):
        rows = slice(r * blk, (r + 1) * blk)
        y = _dot(sel, yg_ref[slot])
        o_ref[rows, :] = _ln(DEEPNORM_ALPHA * x2_ref[rows, :] + y, g_ref[...], b_ref[...])


def _combine(npc_s, psrc_s, pdst_s, ri, rw, lst_row, x2, g3, b3, ys):
    t = x2.shape[0]
    row = lambda w: pl.BlockSpec((RT, w), lambda i, *_: (i, 0))
    const = lambda a: pl.BlockSpec(a.shape, lambda i, *_: (0,) * a.ndim)
    grid_spec = pltpu.PrefetchScalarGridSpec(
        num_scalar_prefetch=3,
        grid=(t // RT,),
        in_specs=[row(LANES), row(LANES), pl.BlockSpec((1, 1, LANES), lambda i, *_: (i, 0, 0)),
                  row(D_MODEL), const(g3), const(b3), pl.BlockSpec(memory_space=pl.ANY)],
        out_specs=row(D_MODEL),
        scratch_shapes=[pltpu.VMEM((2, RUN_ROWS, D_MODEL), bf16), pltpu.SemaphoreType.DMA((2,))],
    )
    return pl.pallas_call(
        _combine_kernel,
        out_shape=jax.ShapeDtypeStruct((t, D_MODEL), f32),
        grid_spec=grid_spec,
        compiler_params=_params(("arbitrary",)),
        name="combine",
    )(npc_s, psrc_s, pdst_s, ri, rw, lst_row, x2, g3, b3, ys)


def _augment_qk(w_qk, scale):
    d = w_qk.shape[0]
    w_h = (w_qk * scale).reshape(d, FOX_HEADS, FOX_HEAD_DIM)
    return jnp.pad(w_h, ((0, 0), (0, 0), (0, AUG_LANES - FOX_HEAD_DIM))).reshape(d, FOX_HEADS * AUG_LANES)


def _bias_lane_tables():
    src = jnp.arange(LANES)[:, None]
    dst = jnp.arange(FOX_HEADS * AUG_LANES)[None, :]
    head, off = dst // AUG_LANES, dst % AUG_LANES - FOX_HEAD_DIM
    in_q = (off >= 0) & (off < N_SPLIT)
    in_k = (off >= N_SPLIT) & (off < 2 * N_SPLIT)
    eq = jnp.where(in_q & (src == N_SPLIT * head + off), 1.0, 0.0)
    ek = jnp.where(in_k & (src == N_SPLIT * head + off - N_SPLIT), -1.0, 0.0)
    return eq.astype(bf16), ek.astype(bf16), in_k.astype(f32), in_q.astype(f32)


def _round_up(a, m):
    return (a + m - 1) // m * m


def kernel(x, mem, ln_in_g, ln_in_b, w_in, b_forget, ln_v_g, ln_v_b, w_spatial, b_spatial, w_branch_a, w_branch_b, w_out, ln1_g, ln1_b, ln_mem_g, ln_mem_b, w_mq, w_mkv, w_mo, ln2_g, ln2_b, w_router, b_router, w_gate_up, b_gate_up, w_down, b_down, ln3_g, ln3_b):
    bsz, seq, d = x.shape
    m_len = mem.shape[1]
    t = bsz * seq
    assert d == D_MODEL and w_in.shape[0] == DEPTH == 1
    assert seq % TQ == 0 and seq % TM_IN == 0 and seq % TM_POST == 0
    assert t % RT == 0 and RT % TM_POST == 0 and RT == TM_EXP
    vec = lambda a: a.reshape(1, -1).astype(f32)
    l = 0
    o = [0]
    for w in (GM_WIDTH, GM_WIDTH, FOX_WIDTH, FOX_WIDTH, FOX_WIDTH, FOX_HEADS, D_MODEL, D_MODEL):
        o.append(o[-1] + w)
    wi = w_in[l]
    w_f = wi[:, o[5]:o[6]]
    wuv = wi[:, o[0]:o[2]].astype(bf16)
    wq = _augment_qk(wi[:, o[2]:o[3]], FOX_HEAD_DIM ** -0.5 * LOG2E).astype(bf16)
    wk = _augment_qk(wi[:, o[3]:o[4]], 1.0).astype(bf16)
    wv_h = wi[:, o[4]:o[5]].T.reshape(FOX_HEADS, FOX_HEAD_DIM, d)
    wvt = jnp.pad(wv_h, ((0, 0), (0, VT_ROWS - FOX_HEAD_DIM), (0, 0))).reshape(FOX_HEADS * VT_ROWS, d).astype(bf16)
    vone = jnp.tile((jnp.arange(VT_ROWS) >= FOX_HEAD_DIM).astype(f32), FOX_HEADS).reshape(-1, 1)
    wg = wi[:, o[6]:o[8]].astype(bf16)
    rep3 = lambda a: jnp.pad(jnp.repeat(a, N_SPLIT, axis=-1), ((0, 0), (0, LANES - N_SPLIT * FOX_HEADS)))
    wf = rep3(w_f).astype(bf16)
    bf3 = rep3(b_forget[l][None, :].astype(f32))
    eq, ek, cq, ck = _bias_lane_tables()
    pos = jnp.arange(GM_BLOCK)
    chunk_mask = (pos[None, :] // CHUNK) <= (pos[:, None] // CHUNK)
    wsp = jnp.where(chunk_mask[None], w_spatial[l], 0).astype(bf16)
    bsp = jnp.repeat(b_spatial[l].T, GM_WIDTH // GM_GROUPS, axis=1).astype(f32)
    wr_f = jnp.pad(w_router[l], ((0, 0), (0, LANES - N_EXPERTS))).astype(f32)
    wr_hi = wr_f.astype(bf16)
    wr = jnp.concatenate([wr_hi, (wr_f - wr_hi.astype(f32)).astype(bf16)], axis=1)
    br = jnp.pad(b_router[l], (0, LANES - N_EXPERTS)).reshape(1, LANES).astype(f32)

    k_mem, v_mem = _mem_kv(mem.reshape(bsz * m_len, d), vec(ln_mem_g[l]), vec(ln_mem_b[l]),
                           w_mkv[l].astype(bf16), m_len)
    xn, gu, vn, qt_aug, k_aug, vt, sa, sb = _in_proj(
        x.reshape(t, d), vec(ln_in_g), vec(ln_in_b), wuv, wq, wk, wvt, vone, wg, wf, bf3, eq, ek, cq, ck,
        vec(ln_v_g[l]), vec(ln_v_b[l]), seq)
    yb = _fox_attn(qt_aug, k_aug.reshape(bsz, seq, -1), vt, bsz, seq)
    x2, x2b, ri, rw, cnt_tiles = _post_mix(
        gu, vn, sa, sb, yb.reshape(t, FOX_WIDTH), xn, k_mem, v_mem, wsp, bsp,
        w_branch_a[l].astype(bf16), w_branch_b[l].astype(bf16), w_out[l].astype(bf16),
        vec(ln1_g[l]), vec(ln1_b[l]), w_mq[l].astype(bf16), w_mo[l].astype(bf16),
        vec(ln2_g[l]), vec(ln2_b[l]), wr, br, seq, m_len)

    i32 = jnp.int32
    n_rt = t // RT
    cnt = cnt_tiles[:, 0, :N_EXPERTS].astype(i32).reshape(n_rt, RT // TM_POST, N_EXPERTS).sum(axis=1)
    cnt_pad = _round_up(cnt, ROW_ALIGN)
    lst = jnp.cumsum(cnt_pad, axis=1) - cnt_pad
    rows_e = cnt_pad.sum(axis=0)
    region = _round_up(rows_e, TM_EXP)
    e_end = jnp.cumsum(region)
    e_off = e_end - region
    dst = e_off[None, :] + jnp.cumsum(cnt_pad, axis=0) - cnt_pad
    n_tiles = -(-(t * TOP_K + n_rt * N_EXPERTS * (ROW_ALIGN - 1)) // TM_EXP) + N_EXPERTS
    n_used = (e_end[-1:] // TM_EXP).astype(i32)
    tile_ids = jnp.arange(n_tiles, dtype=i32)
    e_t = jnp.minimum(jnp.sum(tile_ids[:, None] * TM_EXP >= e_end[None, :], axis=1), N_EXPERTS - 1).astype(i32)
    newexp = jnp.concatenate([jnp.ones((1,), i32), (e_t[1:] != e_t[:-1]).astype(i32)])
    n_tail = n_tiles - t * TOP_K // TM_EXP
    tail_tile = n_used[0] + jnp.arange(n_tail, dtype=i32)
    z_dst = jnp.concatenate([e_off + rows_e, tail_tile * TM_EXP])
    z_n = jnp.concatenate([region - rows_e, jnp.where(tail_tile < n_tiles, TM_EXP, 0)])
    units = lambda a: (a.reshape(-1) // ROW_ALIGN).astype(i32)
    lst_pad = jnp.pad(lst, ((0, 0), (0, LANES - N_EXPERTS))).astype(f32)
    n_u, lst_u, dst_u = cnt_pad // ROW_ALIGN, lst // ROW_ALIGN, dst // ROW_ALIGN
    sizes = jnp.asarray(PIECE_SIZES, i32)[None, :, None]
    has = ((n_u[:, None, :] & sizes) != 0).astype(i32)
    done_u = n_u[:, None, :] // (2 * sizes) * (2 * sizes)
    slot_of = jnp.cumsum(has, axis=-1) - 1
    put = has[..., None, :] * (slot_of[..., None, :] == jnp.arange(N_EXPERTS, dtype=i32)[:, None])
    listed = lambda a: jnp.sum(put * (a[:, None, :] + done_u)[..., None, :], axis=-1).reshape(-1).astype(i32)
    npc = has.sum(axis=-1).reshape(-1).astype(i32)
    psrc, pdst = listed(lst_u), listed(dst_u)

    xs = _dispatch(npc, psrc, pdst, units(z_n), units(z_dst), ri, lst_pad[:, :, None], x2b, n_tiles * TM_EXP)
    ys = _experts(e_t, newexp, n_used, xs, w_gate_up[l], b_gate_up[l][:, None, :].astype(f32),
                  w_down[l], b_down[l][:, None, :].astype(f32))
    out = _combine(npc, psrc, pdst, ri, rw, lst_pad[:, None, :], x2, vec(ln3_g[l]), vec(ln3_b[l]), ys)
    return out.reshape(bsz, seq, d)
```

```python
import functools
import math

import jax
import jax.numpy as jnp
from jax import lax
from jax.experimental import pallas as pl
from jax.experimental.pallas import tpu as pltpu

D_MODEL = 1024
CHUNK = 64
GM_WIDTH = 512
GM_GROUPS = 8
GM_BLOCK = 128
FOX_HEADS = 8
FOX_HEAD_DIM = 64
FOX_WIDTH = FOX_HEADS * FOX_HEAD_DIM
MEM_HEADS = 4
MEM_HEAD_DIM = D_MODEL // MEM_HEADS
N_EXPERTS = 32
TOP_K = 4
D_EXPERT = D_MODEL
SWIGLU_ALPHA = 1.702
SWIGLU_LIMIT = 7.0
LN_EPS = 1e-5
DEPTH = 1
DEEPNORM_ALPHA = (2 * DEPTH) ** 0.25

LANES = 128
ROW_ALIGN = 16
AUG_LANES = 2 * FOX_HEAD_DIM
N_SPLIT = 3
VMEM_LIMIT = 56 * 1024 * 1024
NEG_BIG = -1e30
LOG2E = math.log2(math.e)

TM_IN = 512
TQ = 1024
TK = 512
ATT_HEADS = 4
ATT_UNROLL = 4
VT_ROWS = FOX_HEAD_DIM + 16
TM_POST = 512
POST_SUB = 256
RT = 512
TM_EXP = 512
RUN_ROWS = TOP_K * RT + N_EXPERTS * ROW_ALIGN
MAX_RUN_UNITS = RT // ROW_ALIGN
SEL_BLOCKS = 2
PIECE_SIZES = tuple(MAX_RUN_UNITS >> b for b in range(MAX_RUN_UNITS.bit_length()))

bf16 = jnp.bfloat16
f32 = jnp.float32


def _ln(x, g, b):
    mu = jnp.mean(x, axis=-1, keepdims=True)
    xc = x - mu
    var = jnp.mean(xc * xc, axis=-1, keepdims=True)
    return xc * lax.rsqrt(var + LN_EPS) * g + b


def _dot(a, b):
    return jnp.dot(a, b, preferred_element_type=f32)


def _dot_nt(a, b):
    return lax.dot_general(a, b, (((1,), (1,)), ((), ())), preferred_element_type=f32)


def _params(sem):
    return pltpu.CompilerParams(dimension_semantics=sem, vmem_limit_bytes=VMEM_LIMIT)


def _const_spec(shape):
    nd = len(shape)
    return pl.BlockSpec(shape, lambda *_: (0,) * nd, pipeline_mode=pl.Buffered(1))


def _mem_kv_kernel(mem_ref, g_ref, b_ref, w_ref, k_ref, v_ref):
    mn = _ln(mem_ref[...], g_ref[...], b_ref[...]).astype(bf16)
    kv = _dot(mn, w_ref[...])
    k_ref[...] = kv[:, :D_MODEL].astype(bf16)
    v_ref[...] = kv[:, D_MODEL:].astype(bf16)


def _mem_kv(mem2, g, b, w_mkv, m_len):
    rows = mem2.shape[0]
    return pl.pallas_call(
        _mem_kv_kernel,
        out_shape=(jax.ShapeDtypeStruct((rows, D_MODEL), bf16),) * 2,
        grid=(rows // m_len,),
        in_specs=[pl.BlockSpec((m_len, D_MODEL), lambda i: (i, 0)),
                  _const_spec((1, D_MODEL)), _const_spec((1, D_MODEL)),
                  _const_spec((D_MODEL, 2 * D_MODEL))],
        out_specs=(pl.BlockSpec((m_len, D_MODEL), lambda i: (i, 0)),) * 2,
        compiler_params=_params(("arbitrary",)),
        name="mem_kv",
    )(mem2, g, b, w_mkv)


def _in_proj_kernel(tiles_per_seq, x_ref, g_ref, b_ref, wuv_ref, wq_ref, wk_ref, wvt_ref, vone_ref, wg_ref,
                    wf_ref, bf_ref, eq_ref, ek_ref, cq_ref, ck_ref, lvg_ref, lvb_ref,
                    xn_ref, gu_ref, vn_ref, qt_ref, k_ref, vt_ref, sa_ref, sb_ref, carry_ref):
    tm = x_ref.shape[0]
    xn = _ln(x_ref[...], g_ref[...], b_ref[...])
    xn_ref[...] = xn
    xb = xn.astype(bf16)

    huv = _dot(xb, wuv_ref[...])
    gel = 0.5 * huv * (1.0 + lax.erf(huv * (1.0 / math.sqrt(2.0))))
    gu_ref[...] = gel[:, :GM_WIDTH].astype(bf16)
    vn_ref[...] = _ln(gel[:, GM_WIDTH:], lvg_ref[...], lvb_ref[...]).astype(bf16)

    vt_ref[...] = (_dot_nt(wvt_ref[...], xb) + vone_ref[...]).astype(bf16)

    hg = _dot(xb, wg_ref[...])
    sg = jax.nn.sigmoid(hg)
    sa_ref[...] = sg[:, :D_MODEL].astype(bf16)
    sb_ref[...] = sg[:, D_MODEL:].astype(bf16)

    z = _dot(xb, wf_ref[...]) + bf_ref[...]
    lane = lax.broadcasted_iota(jnp.int32, z.shape, 1)
    row = lax.broadcasted_iota(jnp.int32, z.shape, 0)
    log_f = jnp.minimum(z, 0.0) - jnp.log1p(jnp.exp(-jnp.abs(z)))
    c = jnp.where(lane < N_SPLIT * FOX_HEADS, log_f * LOG2E, 0.0)
    shift = 1
    while shift < tm:
        c = c + jnp.where(row >= shift, pltpu.roll(c, shift, 0), 0.0)
        shift *= 2

    @pl.when(pl.program_id(0) % tiles_per_seq == 0)
    def _():
        carry_ref[...] = jnp.zeros_like(carry_ref)

    c = c + carry_ref[...]
    carry_ref[...] = c[tm - 1:tm, :]
    hi = c.astype(bf16).astype(f32)
    r1 = c - hi
    mid = r1.astype(bf16).astype(f32)
    lo = r1 - mid
    piece = jnp.where(lane % N_SPLIT == 0, hi, jnp.where(lane % N_SPLIT == 1, mid, lo)).astype(bf16)
    q_aug = _dot(xb, wq_ref[...]) + _dot(piece, eq_ref[...]) + cq_ref[...]
    qt_ref[...] = jnp.transpose(q_aug).astype(bf16)
    k_ref[...] = (_dot(xb, wk_ref[...]) + _dot(piece, ek_ref[...]) + ck_ref[...]).astype(bf16)


def _in_proj(x2, g, b, wuv, wq, wk, wvt, vone, wg, wf, bf3, eq, ek, cq, ck, lvg, lvb, seq):
    t = x2.shape[0]
    tm = TM_IN
    row = lambda w, dt: (jax.ShapeDtypeStruct((t, w), dt), pl.BlockSpec((tm, w), lambda i: (i, 0)))
    col = lambda h, dt: (jax.ShapeDtypeStruct((h, t), dt), pl.BlockSpec((h, tm), lambda i: (0, i)))
    outs = [row(D_MODEL, f32), row(GM_WIDTH, bf16), row(GM_WIDTH, bf16), col(FOX_HEADS * AUG_LANES, bf16),
            row(FOX_HEADS * AUG_LANES, bf16), col(wvt.shape[0], bf16), row(D_MODEL, bf16), row(D_MODEL, bf16)]
    consts = [g, b, wuv, wq, wk, wvt, vone, wg, wf, bf3, eq, ek, cq, ck, lvg, lvb]
    return pl.pallas_call(
        functools.partial(_in_proj_kernel, seq // tm),
        out_shape=tuple(o[0] for o in outs),
        grid=(t // tm,),
        in_specs=[pl.BlockSpec((tm, D_MODEL), lambda i: (i, 0))] + [_const_spec(c.shape) for c in consts],
        out_specs=tuple(o[1] for o in outs),
        scratch_shapes=[pltpu.VMEM((1, LANES), f32)],
        compiler_params=_params(("arbitrary",)),
        name="in_proj",
    )(x2, *consts)


def _fox_attn_kernel(qt_ref, k_ref, vt_ref, o_ref):
    i = pl.program_id(2)
    tq = qt_ref.shape[1]
    key_ids = lax.broadcasted_iota(jnp.int32, (TK, tq), 0)
    qry_ids = lax.broadcasted_iota(jnp.int32, (TK, tq), 1)
    heads = range(ATT_HEADS)
    qts = [qt_ref[h * AUG_LANES:(h + 1) * AUG_LANES, :] for h in heads]

    def step(j, carry, mask_off):
        start = pl.multiple_of(j * TK, TK)
        sts = [_dot(k_ref[0, pl.ds(start, TK), h * AUG_LANES:(h + 1) * AUG_LANES], qts[h]) for h in heads]
        soft = []
        for h in heads:
            st = sts[h] if mask_off is None else jnp.where(key_ids + mask_off <= qry_ids, sts[h], NEG_BIG)
            m_new = jnp.maximum(carry[h][0], jnp.max(st, axis=0, keepdims=True))
            soft.append((m_new, jnp.exp2(carry[h][0] - m_new), jnp.exp2(st - m_new).astype(bf16)))
        new = []
        for h in heads:
            m_new, alpha, p = soft[h]
            vtj = vt_ref[h * VT_ROWS:(h + 1) * VT_ROWS, pl.ds(start, TK)]
            new.append((m_new, alpha * carry[h][1] + _dot(vtj, p)))
        return tuple(new)

    one = (jnp.full((1, tq), NEG_BIG, f32), jnp.zeros((VT_ROWS, tq), f32))
    n_sub = tq // TK
    full = i * n_sub
    carry, done, width = (one,) * ATT_HEADS, 0, ATT_UNROLL
    while width >= 1:
        def group(jj, c, width=width, done=done):
            for u in range(width):
                c = step(done + width * jj + u, c, None)
            return c

        trips = (full - done) // width
        carry = lax.fori_loop(0, trips, group, carry)
        done, width = done + trips * width, width // 2
    for d in range(n_sub):
        carry = step(full + d, carry, d * TK)
    out_t = jnp.concatenate([acc[:FOX_HEAD_DIM] / acc[FOX_HEAD_DIM:FOX_HEAD_DIM + 1] for _, acc in carry], axis=0)
    o_ref[0] = jnp.transpose(out_t).astype(bf16)


def _fox_attn(qt_aug, k_aug, vt, bsz, s):
    assert TQ % TK == 0
    nq = s // TQ
    return pl.pallas_call(
        _fox_attn_kernel,
        out_shape=jax.ShapeDtypeStruct((bsz, s, FOX_WIDTH), bf16),
        grid=(bsz, FOX_HEADS // ATT_HEADS, nq),
        in_specs=[pl.BlockSpec((ATT_HEADS * AUG_LANES, TQ), lambda bi, hp, i: (hp, bi * nq + i)),
                  pl.BlockSpec((1, s, ATT_HEADS * AUG_LANES), lambda bi, hp, i: (bi, 0, hp)),
                  pl.BlockSpec((ATT_HEADS * VT_ROWS, s), lambda bi, hp, i: (hp, bi))],
        out_specs=pl.BlockSpec((1, TQ, ATT_HEADS * FOX_HEAD_DIM), lambda bi, hp, i: (bi, i, hp)),
        compiler_params=_params(("arbitrary", "arbitrary", "arbitrary")),
        name="fox_attn",
    )(qt_aug, k_aug, vt)


def _post_mix_kernel(gu_ref, vn_ref, sa_ref, sb_ref, yb_ref, xn_ref, km_ref, vm_ref,
                     wsp_ref, bsp_ref, wa_ref, wb_ref, wo_ref, g1_ref, b1_ref,
                     wmq_ref, wmo_ref, g2_ref, b2_ref, wr_ref, br_ref,
                     x2_ref, x2b_ref, ri_ref, rw_ref, cnt_ref):
    tm = gu_ref.shape[0]
    subs = [slice(s0, s0 + POST_SUB) for s0 in range(0, tm, POST_SUB)]
    each = lambda fn, *lists: [fn(*args) for args in zip(*lists)] if lists else [fn(sl) for sl in subs]
    lane = lax.broadcasted_iota(jnp.int32, (GM_BLOCK, LANES), 1)

    def spatial(sl):
        blocks = []
        for r0 in range(sl.start, sl.stop, GM_BLOCK):
            pairs = []
            for p in range(GM_GROUPS // 2):
                vp = vn_ref[r0:r0 + GM_BLOCK, p * LANES:(p + 1) * LANES]
                lo = _dot(wsp_ref[2 * p], vp)
                hi = _dot(wsp_ref[2 * p + 1], vp)
                pairs.append(jnp.where(lane < GM_WIDTH // GM_GROUPS, lo, hi))
            blocks.append(jnp.concatenate(pairs, axis=1) + bsp_ref[...])
        return (gu_ref[sl, :].astype(f32) * jnp.concatenate(blocks, axis=0)).astype(bf16)

    y_a = each(spatial)
    br_a = each(lambda y: _dot(y, wa_ref[...]), y_a)
    br_b = each(lambda sl: _dot(yb_ref[sl, :], wb_ref[...]))
    merged = each(lambda sl, a, b: (sa_ref[sl, :].astype(f32) * a + sb_ref[sl, :].astype(f32) * b).astype(bf16),
                  subs, br_a, br_b)
    proj = each(lambda m: _dot(m, wo_ref[...]), merged)
    x1 = each(lambda sl, p: _ln(DEEPNORM_ALPHA * xn_ref[sl, :] + p, g1_ref[...], b1_ref[...]), subs, proj)
    qm = each(lambda v: (_dot(v.astype(bf16), wmq_ref[...]) * (MEM_HEAD_DIM ** -0.5)).astype(bf16), x1)
    heads = [[] for _ in subs]
    for h in range(MEM_HEADS):
        sl_h = slice(h * MEM_HEAD_DIM, (h + 1) * MEM_HEAD_DIM)
        sc = each(lambda q: _dot_nt(q[:, sl_h], km_ref[:, sl_h]), qm)
        pr = each(lambda s: jnp.exp(s - jnp.max(s, axis=-1, keepdims=True)), sc)
        pr = each(lambda p: (p / jnp.sum(p, axis=-1, keepdims=True)).astype(bf16), pr)
        for hs, p in zip(heads, pr):
            hs.append(_dot(p, vm_ref[:, sl_h]))
    o = each(lambda hs: jnp.concatenate(hs, axis=1).astype(bf16), heads)
    proj2 = each(lambda v: _dot(v, wmo_ref[...]), o)
    x2 = each(lambda a, p: _ln(DEEPNORM_ALPHA * a + p, g2_ref[...], b2_ref[...]), x1, proj2)

    def route(sl, v):
        x2_ref[sl, :] = v
        x2b_ref[sl, :] = v.astype(bf16)
        v_hi = v.astype(bf16)
        v_lo = (v - v_hi.astype(f32)).astype(bf16)
        both = _dot(v_hi, wr_ref[...])
        logits = both[:, :LANES] + both[:, LANES:] + _dot(v_lo, wr_ref[:, :LANES]) + br_ref[...]
        n = v.shape[0]
        lane_t = lax.broadcasted_iota(jnp.int32, (n, LANES), 1)
        work = jnp.where(lane_t < N_EXPERTS, logits, -jnp.inf)
        tops, idxs = [], []
        for _ in range(TOP_K):
            mk = jnp.max(work, axis=-1, keepdims=True)
            ik = jnp.min(jnp.where(work == mk, lane_t, LANES), axis=-1, keepdims=True)
            work = jnp.where(lane_t == ik, -jnp.inf, work)
            tops.append(mk)
            idxs.append(ik)
        exps = [jnp.exp(tk - tops[0]) for tk in tops]
        denom = exps[0] + exps[1] + exps[2] + exps[3]
        member = jnp.zeros((n, LANES), f32)
        ri = jnp.zeros((n, LANES), jnp.int32)
        rw = jnp.zeros((n, LANES), f32)
        for k in range(TOP_K):
            member = member + jnp.where(lane_t == idxs[k], 1.0, 0.0)
            ri = jnp.where(lane_t == k, idxs[k], ri)
            rw = jnp.where(lane_t == k, exps[k] / denom, rw)
        ri_ref[sl, :] = ri
        rw_ref[sl, :] = rw
        return jnp.sum(member, axis=0, keepdims=True)

    counts = each(route, subs, x2)
    cnt_ref[0] = functools.reduce(lambda a, b: a + b, counts)


def _post_mix(gu, vn, sa, sb, yb, xn, k_mem, v_mem, wsp, bsp, wa, wb, wo, g1, b1, wmq, wmo, g2, b2, wr, br,
              seq, m_len):
    t = gu.shape[0]
    tm = TM_POST
    per_seq = seq // tm
    row = lambda w: pl.BlockSpec((tm, w), lambda i: (i, 0))
    memspec = pl.BlockSpec((m_len, D_MODEL), lambda i: (i // per_seq, 0))
    consts = [wsp, bsp, wa, wb, wo, g1, b1, wmq, wmo, g2, b2, wr, br]
    return pl.pallas_call(
        _post_mix_kernel,
        out_shape=(jax.ShapeDtypeStruct((t, D_MODEL), f32),
                   jax.ShapeDtypeStruct((t, D_MODEL), bf16),
                   jax.ShapeDtypeStruct((t, LANES), jnp.int32),
                   jax.ShapeDtypeStruct((t, LANES), f32),
                   jax.ShapeDtypeStruct((t // tm, 1, LANES), f32)),
        grid=(t // tm,),
        in_specs=[row(GM_WIDTH), row(GM_WIDTH), row(D_MODEL), row(D_MODEL), row(FOX_WIDTH), row(D_MODEL),
                  memspec, memspec] + [_const_spec(c.shape) for c in consts],
        out_specs=(row(D_MODEL), row(D_MODEL), row(LANES), row(LANES),
                   pl.BlockSpec((1, 1, LANES), lambda i: (i, 0, 0))),
        compiler_params=_params(("arbitrary",)),
        name="post_mix",
    )(gu, vn, sa, sb, yb, xn, k_mem, v_mem, *consts)


def _zero_fill_copies(n_s, dst_s, zero_ref, global_ref, sem, op):
    def per_range(r, _):
        n = n_s[r]
        gd = dst_s[r]
        for size in PIECE_SIZES:
            done = (n // (2 * size)) * (2 * size)

            @pl.when((n & size) != 0)
            def _(size=size, done=done):
                go = pl.multiple_of((gd + done) * ROW_ALIGN, ROW_ALIGN)
                cp = pltpu.make_async_copy(zero_ref.at[pl.ds(0, size * ROW_ALIGN)],
                                           global_ref.at[pl.ds(go, size * ROW_ALIGN)], sem)
                cp.start() if op == "start" else cp.wait()

        return 0

    lax.fori_loop(0, n_s.shape[0], per_range, 0)


def _piece_copies(npc_s, psrc_s, pdst_s, tile, local_ref, global_ref, sem, to_global, op):
    for c, size in enumerate(PIECE_SIZES):
        cls = tile * len(PIECE_SIZES) + c

        def one(k, _, size=size, cls=cls):
            lo = pl.multiple_of(psrc_s[cls * N_EXPERTS + k] * ROW_ALIGN, ROW_ALIGN)
            go = pl.multiple_of(pdst_s[cls * N_EXPERTS + k] * ROW_ALIGN, ROW_ALIGN)
            loc = local_ref.at[pl.ds(lo, size * ROW_ALIGN)]
            glo = global_ref.at[pl.ds(go, size * ROW_ALIGN)]
            cp = pltpu.make_async_copy(loc, glo, sem) if to_global else pltpu.make_async_copy(glo, loc, sem)
            cp.start() if op == "start" else cp.wait()
            return 0

        lax.fori_loop(0, npc_s[cls], one, 0)


def _dispatch_kernel(npc_s, psrc_s, pdst_s, zn_s, zdst_s, ri_ref, lstc_ref, x_ref, xs_ref, xg_ref, zero_ref, sem):
    i = pl.program_id(0)
    slot = i % 2
    rt = x_ref.shape[0]
    idx_t = jnp.transpose(ri_ref[...].astype(f32))
    sub = lax.broadcasted_iota(jnp.int32, (LANES, rt), 0).astype(f32)
    member_t = jnp.zeros((LANES, rt), f32)
    for k in range(TOP_K):
        member_t = member_t + jnp.where(sub == idx_t[k:k + 1, :], 1.0, 0.0)
    r_ids = lax.broadcasted_iota(jnp.int32, (rt, rt), 0)
    c_ids = lax.broadcasted_iota(jnp.int32, (rt, rt), 1)
    earlier = jnp.where(r_ids < c_ids, 1.0, 0.0).astype(bf16)
    base = lstc_ref[0] + _dot(member_t.astype(bf16), earlier)
    lpos = [jnp.sum(jnp.where(sub == idx_t[k:k + 1, :], base, 0.0), axis=0, keepdims=True).astype(jnp.int32)
            for k in range(TOP_K)]
    blk = RUN_ROWS // SEL_BLOCKS
    row_i = lax.broadcasted_iota(jnp.int32, (blk, rt), 0)
    sels = []
    for r in range(SEL_BLOCKS):
        sel = jnp.zeros((blk, rt), f32)
        for k in range(TOP_K):
            sel = jnp.where(row_i == lpos[k] - r * blk, 1.0, sel)
        sels.append(sel.astype(bf16))
    for r, sel in enumerate(sels):
        xg_ref[slot, r * blk:(r + 1) * blk, :] = _dot(sel, x_ref[...]).astype(bf16)
    _piece_copies(npc_s, psrc_s, pdst_s, i, xg_ref.at[slot], xs_ref, sem.at[slot], True, "start")

    @pl.when(i > 0)
    def _():
        _piece_copies(npc_s, psrc_s, pdst_s, i - 1, xg_ref.at[1 - slot], xs_ref, sem.at[1 - slot], True, "wait")

    @pl.when(i == pl.num_programs(0) - 1)
    def _():
        zero_ref[...] = jnp.zeros_like(zero_ref)
        _zero_fill_copies(zn_s, zdst_s, zero_ref, xs_ref, sem.at[2], "start")
        _zero_fill_copies(zn_s, zdst_s, zero_ref, xs_ref, sem.at[2], "wait")
        _piece_copies(npc_s, psrc_s, pdst_s, i, xg_ref.at[slot], xs_ref, sem.at[slot], True, "wait")


def _dispatch(npc_s, psrc_s, pdst_s, zn_s, zdst_s, ri, lst_col, x2b, n_rows):
    t = x2b.shape[0]
    grid_spec = pltpu.PrefetchScalarGridSpec(
        num_scalar_prefetch=5,
        grid=(t // RT,),
        in_specs=[pl.BlockSpec((RT, LANES), lambda i, *_: (i, 0)),
                  pl.BlockSpec((1, LANES, 1), lambda i, *_: (i, 0, 0)),
                  pl.BlockSpec((RT, D_MODEL), lambda i, *_: (i, 0))],
        out_specs=pl.BlockSpec(memory_space=pl.ANY),
        scratch_shapes=[pltpu.VMEM((2, RUN_ROWS, D_MODEL), bf16), pltpu.VMEM((RT, D_MODEL), bf16),
                        pltpu.SemaphoreType.DMA((3,))],
    )
    return pl.pallas_call(
        _dispatch_kernel,
        out_shape=jax.ShapeDtypeStruct((n_rows, D_MODEL), bf16),
        grid_spec=grid_spec,
        compiler_params=_params(("arbitrary",)),
        name="dispatch",
    )(npc_s, psrc_s, pdst_s, zn_s, zdst_s, ri, lst_col, x2b)


def _experts_kernel(et_s, newexp_s, nu_s, xs_ref, wgu_ref, bgu_ref, wd_ref, bd_ref, ys_ref, wgu_bf, wd_bf):
    w = pl.program_id(0)

    @pl.when(w < nu_s[0])
    def _():
        @pl.when(newexp_s[w] == 1)
        def _():
            wgu_bf[...] = wgu_ref[0].astype(bf16)
            wd_bf[...] = wd_ref[0].astype(bf16)

        h = _dot(xs_ref[...], wgu_bf[...]) + bgu_ref[0]
        gate = jnp.minimum(h[:, :D_EXPERT], SWIGLU_LIMIT)
        up = jnp.clip(h[:, D_EXPERT:], -SWIGLU_LIMIT, SWIGLU_LIMIT)
        glu = gate / (1.0 + jnp.exp2(gate * (-SWIGLU_ALPHA * LOG2E)))
        ys_ref[...] = (_dot(((up + 1.0) * glu).astype(bf16), wd_bf[...]) + bd_ref[0]).astype(bf16)

    @pl.when(w >= nu_s[0])
    def _():
        ys_ref[...] = jnp.zeros_like(ys_ref)


def _experts(et, newexp, n_used, xs, wgu, bgu, wd, bd):
    n_rows = xs.shape[0]
    tm = TM_EXP
    used = lambda w, nu: jnp.minimum(w, nu[0] - 1)
    wsel = lambda w, et, newexp, nu: (et[used(w, nu)], 0, 0)
    grid_spec = pltpu.PrefetchScalarGridSpec(
        num_scalar_prefetch=3,
        grid=(n_rows // tm,),
        in_specs=[pl.BlockSpec((tm, D_MODEL), lambda w, et, newexp, nu: (used(w, nu), 0)),
                  pl.BlockSpec((1, D_MODEL, 2 * D_EXPERT), wsel),
                  pl.BlockSpec((1, 1, 2 * D_EXPERT), wsel),
                  pl.BlockSpec((1, D_EXPERT, D_MODEL), wsel),
                  pl.BlockSpec((1, 1, D_MODEL), wsel)],
        out_specs=pl.BlockSpec((tm, D_MODEL), lambda w, et, newexp, nu: (w, 0)),
        scratch_shapes=[pltpu.VMEM((D_MODEL, 2 * D_EXPERT), bf16), pltpu.VMEM((D_EXPERT, D_MODEL), bf16)],
    )
    return pl.pallas_call(
        _experts_kernel,
        out_shape=jax.ShapeDtypeStruct((n_rows, D_MODEL), bf16),
        grid_spec=grid_spec,
        compiler_params=_params(("arbitrary",)),
        name="experts",
    )(et, newexp, n_used, xs, wgu, bgu, wd, bd)


def _combine_kernel(npc_s, psrc_s, pdst_s, ri_ref, rw_ref, lstr_ref, x2_ref, g_ref, b_ref, ys_ref, o_ref,
                    yg_ref, sem):
    i = pl.program_id(0)
    slot = i % 2
    rt = x2_ref.shape[0]

    @pl.when(i == 0)
    def _():
        yg_ref[...] = jnp.zeros_like(yg_ref)
        _piece_copies(npc_s, psrc_s, pdst_s, i, yg_ref.at[slot], ys_ref, sem.at[slot], False, "start")

    @pl.when(i + 1 < pl.num_programs(0))
    def _():
        _piece_copies(npc_s, psrc_s, pdst_s, i + 1, yg_ref.at[1 - slot], ys_ref, sem.at[1 - slot], False, "start")

    _piece_copies(npc_s, psrc_s, pdst_s, i, yg_ref.at[slot], ys_ref, sem.at[slot], False, "wait")

    idx = ri_ref[...]
    rw = rw_ref[...]
    lane = lax.broadcasted_iota(jnp.int32, (rt, LANES), 1)
    member = jnp.zeros((rt, LANES), f32)
    for k in range(TOP_K):
        member = member + jnp.where(lane == idx[:, k:k + 1], 1.0, 0.0)
    r_ids = lax.broadcasted_iota(jnp.int32, (rt, rt), 0)
    c_ids = lax.broadcasted_iota(jnp.int32, (rt, rt), 1)
    earlier = jnp.where(r_ids > c_ids, 1.0, 0.0).astype(bf16)
    base = lstr_ref[0] + _dot(earlier, member.astype(bf16))
    lpos = [jnp.sum(jnp.where(lane == idx[:, k:k + 1], base, 0.0), axis=-1, keepdims=True).astype(jnp.int32)
            for k in range(TOP_K)]
    blk = rt // SEL_BLOCKS
    lane_w = lax.broadcasted_iota(jnp.int32, (blk, RUN_ROWS), 1)
    sels = []
    for r in range(SEL_BLOCKS):
        rows = slice(r * blk, (r + 1) * blk)
        sel = jnp.zeros((blk, RUN_ROWS), f32)
        for k in range(TOP_K):
            sel = jnp.where(lane_w == lpos[k][rows], rw[rows, k:k + 1], sel)
        sels.append(sel.astype(bf16))
    for r, sel in enumerate(sels):
        rows = slice(r * blk, (r + 1) * blk)
        y = _dot(sel, yg_ref[slot])
        o_ref[rows, :] = _ln(DEEPNORM_ALPHA * x2_ref[rows, :] + y, g_ref[...], b_ref[...])


def _combine(npc_s, psrc_s, pdst_s, ri, rw, lst_row, x2, g3, b3, ys):
    t = x2.shape[0]
    row = lambda w: pl.BlockSpec((RT, w), lambda i, *_: (i, 0))
    const = lambda a: pl.BlockSpec(a.shape, lambda i, *_: (0,) * a.ndim)
    grid_spec = pltpu.PrefetchScalarGridSpec(
        num_scalar_prefetch=3,
        grid=(t // RT,),
        in_specs=[row(LANES), row(LANES), pl.BlockSpec((1, 1, LANES), lambda i, *_: (i, 0, 0)),
                  row(D_MODEL), const(g3), const(b3), pl.BlockSpec(memory_space=pl.ANY)],
        out_specs=row(D_MODEL),
        scratch_shapes=[pltpu.VMEM((2, RUN_ROWS, D_MODEL), bf16), pltpu.SemaphoreType.DMA((2,))],
    )
    return pl.pallas_call(
        _combine_kernel,
        out_shape=jax.ShapeDtypeStruct((t, D_MODEL), f32),
        grid_spec=grid_spec,
        compiler_params=_params(("arbitrary",)),
        name="combine",
    )(npc_s, psrc_s, pdst_s, ri, rw, lst_row, x2, g3, b3, ys)


def _augment_qk(w_qk, scale):
    d = w_qk.shape[0]
    w_h = (w_qk * scale).reshape(d, FOX_HEADS, FOX_HEAD_DIM)
    return jnp.pad(w_h, ((0, 0), (0, 0), (0, AUG_LANES - FOX_HEAD_DIM))).reshape(d, FOX_HEADS * AUG_LANES)


def _bias_lane_tables():
    src = jnp.arange(LANES)[:, None]
    dst = jnp.arange(FOX_HEADS * AUG_LANES)[None, :]
    head, off = dst // AUG_LANES, dst % AUG_LANES - FOX_HEAD_DIM
    in_q = (off >= 0) & (off < N_SPLIT)
    in_k = (off >= N_SPLIT) & (off < 2 * N_SPLIT)
    eq = jnp.where(in_q & (src == N_SPLIT * head + off), 1.0, 0.0)
    ek = jnp.where(in_k & (src == N_SPLIT * head + off - N_SPLIT), -1.0, 0.0)
    return eq.astype(bf16), ek.astype(bf16), in_k.astype(f32), in_q.astype(f32)


def _round_up(a, m):
    return (a + m - 1) // m * m


def kernel(x, mem, ln_in_g, ln_in_b, w_in, b_forget, ln_v_g, ln_v_b, w_spatial, b_spatial, w_branch_a, w_branch_b, w_out, ln1_g, ln1_b, ln_mem_g, ln_mem_b, w_mq, w_mkv, w_mo, ln2_g, ln2_b, w_router, b_router, w_gate_up, b_gate_up, w_down, b_down, ln3_g, ln3_b):
    bsz, seq, d = x.shape
    m_len = mem.shape[1]
    t = bsz * seq
    assert d == D_MODEL and w_in.shape[0] == DEPTH == 1
    assert seq % TQ == 0 and seq % TM_IN == 0 and seq % TM_POST == 0
    assert t % RT == 0 and RT % TM_POST == 0 and RT == TM_EXP
    vec = lambda a: a.reshape(1, -1).astype(f32)
    l = 0
    o = [0]
    for w in (GM_WIDTH, GM_WIDTH, FOX_WIDTH, FOX_WIDTH, FOX_WIDTH, FOX_HEADS, D_MODEL, D_MODEL):
        o.append(o[-1] + w)
    wi = w_in[l]
    w_f = wi[:, o[5]:o[6]]
    wuv = wi[:, o[0]:o[2]].astype(bf16)
    wq = _augment_qk(wi[:, o[2]:o[3]], FOX_HEAD_DIM ** -0.5 * LOG2E).astype(bf16)
    wk = _augment_qk(wi[:, o[3]:o[4]], 1.0).astype(bf16)
    wv_h = wi[:, o[4]:o[5]].T.reshape(FOX_HEADS, FOX_HEAD_DIM, d)
    wvt = jnp.pad(wv_h, ((0, 0), (0, VT_ROWS - FOX_HEAD_DIM), (0, 0))).reshape(FOX_HEADS * VT_ROWS, d).astype(bf16)
    vone = jnp.tile((jnp.arange(VT_ROWS) >= FOX_HEAD_DIM).astype(f32), FOX_HEADS).reshape(-1, 1)
    wg = wi[:, o[6]:o[8]].astype(bf16)
    rep3 = lambda a: jnp.pad(jnp.repeat(a, N_SPLIT, axis=-1), ((0, 0), (0, LANES - N_SPLIT * FOX_HEADS)))
    wf = rep3(w_f).astype(bf16)
    bf3 = rep3(b_forget[l][None, :].astype(f32))
    eq, ek, cq, ck = _bias_lane_tables()
    pos = jnp.arange(GM_BLOCK)
    chunk_mask = (pos[None, :] // CHUNK) <= (pos[:, None] // CHUNK)
    wsp = jnp.where(chunk_mask[None], w_spatial[l], 0).astype(bf16)
    bsp = jnp.repeat(b_spatial[l].T, GM_WIDTH // GM_GROUPS, axis=1).astype(f32)
    wr_f = jnp.pad(w_router[l], ((0, 0), (0, LANES - N_EXPERTS))).astype(f32)
    wr_hi = wr_f.astype(bf16)
    wr = jnp.concatenate([wr_hi, (wr_f - wr_hi.astype(f32)).astype(bf16)], axis=1)
    br = jnp.pad(b_router[l], (0, LANES - N_EXPERTS)).reshape(1, LANES).astype(f32)

    k_mem, v_mem = _mem_kv(mem.reshape(bsz * m_len, d), vec(ln_mem_g[l]), vec(ln_mem_b[l]),
                           w_mkv[l].astype(bf16), m_len)
    xn, gu, vn, qt_aug, k_aug, vt, sa, sb = _in_proj(
        x.reshape(t, d), vec(ln_in_g), vec(ln_in_b), wuv, wq, wk, wvt, vone, wg, wf, bf3, eq, ek, cq, ck,
        vec(ln_v_g[l]), vec(ln_v_b[l]), seq)
    yb = _fox_attn(qt_aug, k_aug.reshape(bsz, seq, -1), vt, bsz, seq)
    x2, x2b, ri, rw, cnt_tiles = _post_mix(
        gu, vn, sa, sb, yb.reshape(t, FOX_WIDTH), xn, k_mem, v_mem, wsp, bsp,
        w_branch_a[l].astype(bf16), w_branch_b[l].astype(bf16), w_out[l].astype(bf16),
        vec(ln1_g[l]), vec(ln1_b[l]), w_mq[l].astype(bf16), w_mo[l].astype(bf16),
        vec(ln2_g[l]), vec(ln2_b[l]), wr, br, seq, m_len)

    i32 = jnp.int32
    n_rt = t // RT
    cnt = cnt_tiles[:, 0, :N_EXPERTS].astype(i32).reshape(n_rt, RT // TM_POST, N_EXPERTS).sum(axis=1)
    cnt_pad = _round_up(cnt, ROW_ALIGN)
    lst = jnp.cumsum(cnt_pad, axis=1) - cnt_pad
    rows_e = cnt_pad.sum(axis=0)
    region = _round_up(rows_e, TM_EXP)
    e_end = jnp.cumsum(region)
    e_off = e_end - region
    dst = e_off[None, :] + jnp.cumsum(cnt_pad, axis=0) - cnt_pad
    n_tiles = -(-(t * TOP_K + n_rt * N_EXPERTS * (ROW_ALIGN - 1)) // TM_EXP) + N_EXPERTS
    n_used = (e_end[-1:] // TM_EXP).astype(i32)
    tile_ids = jnp.arange(n_tiles, dtype=i32)
    e_t = jnp.minimum(jnp.sum(tile_ids[:, None] * TM_EXP >= e_end[None, :], axis=1), N_EXPERTS - 1).astype(i32)
    newexp = jnp.concatenate([jnp.ones((1,), i32), (e_t[1:] != e_t[:-1]).astype(i32)])
    n_tail = n_tiles - t * TOP_K // TM_EXP
    tail_tile = n_used[0] + jnp.arange(n_tail, dtype=i32)
    z_dst = jnp.concatenate([e_off + rows_e, tail_tile * TM_EXP])
    z_n = jnp.concatenate([region - rows_e, jnp.where(tail_tile < n_tiles, TM_EXP, 0)])
    units = lambda a: (a.reshape(-1) // ROW_ALIGN).astype(i32)
    lst_pad = jnp.pad(lst, ((0, 0), (0, LANES - N_EXPERTS))).astype(f32)
    n_u, lst_u, dst_u = cnt_pad // ROW_ALIGN, lst // ROW_ALIGN, dst // ROW_ALIGN
    sizes = jnp.asarray(PIECE_SIZES, i32)[None, :, None]
    has = ((n_u[:, None, :] & sizes) != 0).astype(i32)
    done_u = n_u[:, None, :] // (2 * sizes) * (2 * sizes)
    slot_of = jnp.cumsum(has, axis=-1) - 1
    put = has[..., None, :] * (slot_of[..., None, :] == jnp.arange(N_EXPERTS, dtype=i32)[:, None])
    listed = lambda a: jnp.sum(put * (a[:, None, :] + done_u)[..., None, :], axis=-1).reshape(-1).astype(i32)
    npc = has.sum(axis=-1).reshape(-1).astype(i32)
    psrc, pdst = listed(lst_u), listed(dst_u)

    xs = _dispatch(npc, psrc, pdst, units(z_n), units(z_dst), ri, lst_pad[:, :, None], x2b, n_tiles * TM_EXP)
    ys = _experts(e_t, newexp, n_used, xs, w_gate_up[l], b_gate_up[l][:, None, :].astype(f32),
                  w_down[l], b_down[l][:, None, :].astype(f32))
    out = _combine(npc, psrc, pdst, ri, rw, lst_pad[:, None, :], x2, vec(ln3_g[l]), vec(ln3_b[l]), ys)
    return out.reshape(bsz, seq, d)
```

```python
import functools
import math

import jax
import jax.numpy as jnp
from jax import lax
from jax.experimental import pallas as pl
from jax.experimental.pallas import tpu as pltpu

D_MODEL = 1024
CHUNK = 64
GM_WIDTH = 512
GM_GROUPS = 8
GM_BLOCK = 128
FOX_HEADS = 8
FOX_HEAD_DIM = 64
FOX_WIDTH = FOX_HEADS * FOX_HEAD_DIM
MEM_HEADS = 4
MEM_HEAD_DIM = D_MODEL // MEM_HEADS
N_EXPERTS = 32
TOP_K = 4
D_EXPERT = D_MODEL
SWIGLU_ALPHA = 1.702
SWIGLU_LIMIT = 7.0
LN_EPS = 1e-5
DEPTH = 1
DEEPNORM_ALPHA = (2 * DEPTH) ** 0.25

LANES = 128
ROW_ALIGN = 16
AUG_LANES = 2 * FOX_HEAD_DIM
N_SPLIT = 3
VMEM_LIMIT = 56 * 1024 * 1024
NEG_BIG = -1e30
LOG2E = math.log2(math.e)

TM_IN = 512
TQ = 1024
TK = 512
ATT_HEADS = 4
ATT_UNROLL = 4
ATT_QSPLIT = 2
VT_ROWS = FOX_HEAD_DIM + 16
TM_POST = 512
POST_SUB = 256
RT = 512
TM_EXP = 512
RUN_ROWS = TOP_K * RT + N_EXPERTS * ROW_ALIGN
MAX_RUN_UNITS = RT // ROW_ALIGN
SEL_BLOCKS = 2
PIECE_SIZES = tuple(MAX_RUN_UNITS >> b for b in range(MAX_RUN_UNITS.bit_length()))

bf16 = jnp.bfloat16
f32 = jnp.float32


def _ln(x, g, b):
    mu = jnp.mean(x, axis=-1, keepdims=True)
    xc = x - mu
    var = jnp.mean(xc * xc, axis=-1, keepdims=True)
    return xc * lax.rsqrt(var + LN_EPS) * g + b


def _dot(a, b):
    return jnp.dot(a, b, preferred_element_type=f32)


def _dot_nt(a, b):
    return lax.dot_general(a, b, (((1,), (1,)), ((), ())), preferred_element_type=f32)


def _params(sem):
    return pltpu.CompilerParams(dimension_semantics=sem, vmem_limit_bytes=VMEM_LIMIT)


def _const_spec(shape):
    nd = len(shape)
    return pl.BlockSpec(shape, lambda *_: (0,) * nd, pipeline_mode=pl.Buffered(1))


def _mem_kv_kernel(mem_ref, g_ref, b_ref, w_ref, k_ref, v_ref):
    mn = _ln(mem_ref[...], g_ref[...], b_ref[...]).astype(bf16)
    kv = _dot(mn, w_ref[...])
    k_ref[...] = kv[:, :D_MODEL].astype(bf16)
    v_ref[...] = kv[:, D_MODEL:].astype(bf16)


def _mem_kv(mem2, g, b, w_mkv, m_len):
    rows = mem2.shape[0]
    return pl.pallas_call(
        _mem_kv_kernel,
        out_shape=(jax.ShapeDtypeStruct((rows, D_MODEL), bf16),) * 2,
        grid=(rows // m_len,),
        in_specs=[pl.BlockSpec((m_len, D_MODEL), lambda i: (i, 0)),
                  _const_spec((1, D_MODEL)), _const_spec((1, D_MODEL)),
                  _const_spec((D_MODEL, 2 * D_MODEL))],
        out_specs=(pl.BlockSpec((m_len, D_MODEL), lambda i: (i, 0)),) * 2,
        compiler_params=_params(("arbitrary",)),
        name="mem_kv",
    )(mem2, g, b, w_mkv)


def _in_proj_kernel(tiles_per_seq, x_ref, g_ref, b_ref, wuv_ref, wq_ref, wk_ref, wvt_ref, vone_ref, wg_ref,
                    wf_ref, bf_ref, eq_ref, ek_ref, cq_ref, ck_ref, lvg_ref, lvb_ref,
                    xn_ref, gu_ref, vn_ref, qt_ref, k_ref, vt_ref, sa_ref, sb_ref, carry_ref):
    tm = x_ref.shape[0]
    xn = _ln(x_ref[...], g_ref[...], b_ref[...])
    xn_ref[...] = xn
    xb = xn.astype(bf16)

    huv = _dot(xb, wuv_ref[...])
    gel = 0.5 * huv * (1.0 + lax.erf(huv * (1.0 / math.sqrt(2.0))))
    gu_ref[...] = gel[:, :GM_WIDTH].astype(bf16)
    vn_ref[...] = _ln(gel[:, GM_WIDTH:], lvg_ref[...], lvb_ref[...]).astype(bf16)

    vt_ref[...] = (_dot_nt(wvt_ref[...], xb) + vone_ref[...]).astype(bf16)

    hg = _dot(xb, wg_ref[...])
    sg = jax.nn.sigmoid(hg)
    sa_ref[...] = sg[:, :D_MODEL].astype(bf16)
    sb_ref[...] = sg[:, D_MODEL:].astype(bf16)

    z = _dot(xb, wf_ref[...]) + bf_ref[...]
    lane = lax.broadcasted_iota(jnp.int32, z.shape, 1)
    row = lax.broadcasted_iota(jnp.int32, z.shape, 0)
    log_f = jnp.minimum(z, 0.0) - jnp.log1p(jnp.exp(-jnp.abs(z)))
    c = jnp.where(lane < N_SPLIT * FOX_HEADS, log_f * LOG2E, 0.0)
    shift = 1
    while shift < tm:
        c = c + jnp.where(row >= shift, pltpu.roll(c, shift, 0), 0.0)
        shift *= 2

    @pl.when(pl.program_id(0) % tiles_per_seq == 0)
    def _():
        carry_ref[...] = jnp.zeros_like(carry_ref)

    c = c + carry_ref[...]
    carry_ref[...] = c[tm - 1:tm, :]
    hi = c.astype(bf16).astype(f32)
    r1 = c - hi
    mid = r1.astype(bf16).astype(f32)
    lo = r1 - mid
    piece = jnp.where(lane % N_SPLIT == 0, hi, jnp.where(lane % N_SPLIT == 1, mid, lo)).astype(bf16)
    q_aug = _dot(xb, wq_ref[...]) + _dot(piece, eq_ref[...]) + cq_ref[...]
    qt_ref[...] = jnp.transpose(q_aug).astype(bf16)
    k_ref[...] = (_dot(xb, wk_ref[...]) + _dot(piece, ek_ref[...]) + ck_ref[...]).astype(bf16)


def _in_proj(x2, g, b, wuv, wq, wk, wvt, vone, wg, wf, bf3, eq, ek, cq, ck, lvg, lvb, seq):
    t = x2.shape[0]
    tm = TM_IN
    row = lambda w, dt: (jax.ShapeDtypeStruct((t, w), dt), pl.BlockSpec((tm, w), lambda i: (i, 0)))
    col = lambda h, dt: (jax.ShapeDtypeStruct((h, t), dt), pl.BlockSpec((h, tm), lambda i: (0, i)))
    outs = [row(D_MODEL, f32), row(GM_WIDTH, bf16), row(GM_WIDTH, bf16), col(FOX_HEADS * AUG_LANES, bf16),
            row(FOX_HEADS * AUG_LANES, bf16), col(wvt.shape[0], bf16), row(D_MODEL, bf16), row(D_MODEL, bf16)]
    consts = [g, b, wuv, wq, wk, wvt, vone, wg, wf, bf3, eq, ek, cq, ck, lvg, lvb]
    return pl.pallas_call(
        functools.partial(_in_proj_kernel, seq // tm),
        out_shape=tuple(o[0] for o in outs),
        grid=(t // tm,),
        in_specs=[pl.BlockSpec((tm, D_MODEL), lambda i: (i, 0))] + [_const_spec(c.shape) for c in consts],
        out_specs=tuple(o[1] for o in outs),
        scratch_shapes=[pltpu.VMEM((1, LANES), f32)],
        compiler_params=_params(("arbitrary",)),
        name="in_proj",
    )(x2, *consts)


def _fox_attn_kernel(qt_ref, k_ref, vt_ref, o_ref):
    i = pl.program_id(2)
    tq = qt_ref.shape[1]
    key_ids = lax.broadcasted_iota(jnp.int32, (TK, tq), 0)
    qry_ids = lax.broadcasted_iota(jnp.int32, (TK, tq), 1)
    heads = range(ATT_HEADS)
    qts = [qt_ref[h * AUG_LANES:(h + 1) * AUG_LANES, :] for h in heads]

    def step(j, carry, mask_off):
        start = pl.multiple_of(j * TK, TK)
        sts = [_dot(k_ref[0, pl.ds(start, TK), h * AUG_LANES:(h + 1) * AUG_LANES], qts[h]) for h in heads]
        cols = [slice(c * (tq // ATT_QSPLIT), (c + 1) * (tq // ATT_QSPLIT)) for c in range(ATT_QSPLIT)]
        soft = []
        for h in heads:
            st = sts[h] if mask_off is None else jnp.where(key_ids + mask_off <= qry_ids, sts[h], NEG_BIG)
            for c in cols:
                m_new = jnp.maximum(carry[h][0][:, c], jnp.max(st[:, c], axis=0, keepdims=True))
                soft.append((m_new, jnp.exp2(carry[h][0][:, c] - m_new), jnp.exp2(st[:, c] - m_new).astype(bf16)))
        new = []
        for h in heads:
            vtj = vt_ref[h * VT_ROWS:(h + 1) * VT_ROWS, pl.ds(start, TK)]
            parts = soft[h * ATT_QSPLIT:(h + 1) * ATT_QSPLIT]
            acc = [alpha * carry[h][1][:, c] + _dot(vtj, p) for c, (_, alpha, p) in zip(cols, parts)]
            new.append((jnp.concatenate([m for m, _, _ in parts], axis=1), jnp.concatenate(acc, axis=1)))
        return tuple(new)

    one = (jnp.full((1, tq), NEG_BIG, f32), jnp.zeros((VT_ROWS, tq), f32))
    n_sub = tq // TK
    full = i * n_sub
    carry, done, width = (one,) * ATT_HEADS, 0, ATT_UNROLL
    while width >= 1:
        def group(jj, c, width=width, done=done):
            for u in range(width):
                c = step(done + width * jj + u, c, None)
            return c

        trips = (full - done) // width
        carry = lax.fori_loop(0, trips, group, carry)
        done, width = done + trips * width, width // 2
    for d in range(n_sub):
        carry = step(full + d, carry, d * TK)
    out_t = jnp.concatenate([acc[:FOX_HEAD_DIM] / acc[FOX_HEAD_DIM:FOX_HEAD_DIM + 1] for _, acc in carry], axis=0)
    o_ref[0] = jnp.transpose(out_t).astype(bf16)


def _fox_attn(qt_aug, k_aug, vt, bsz, s):
    assert TQ % TK == 0
    nq = s // TQ
    return pl.pallas_call(
        _fox_attn_kernel,
        out_shape=jax.ShapeDtypeStruct((bsz, s, FOX_WIDTH), bf16),
        grid=(bsz, FOX_HEADS // ATT_HEADS, nq),
        in_specs=[pl.BlockSpec((ATT_HEADS * AUG_LANES, TQ), lambda bi, hp, i: (hp, bi * nq + i)),
                  pl.BlockSpec((1, s, ATT_HEADS * AUG_LANES), lambda bi, hp, i: (bi, 0, hp)),
                  pl.BlockSpec((ATT_HEADS * VT_ROWS, s), lambda bi, hp, i: (hp, bi))],
        out_specs=pl.BlockSpec((1, TQ, ATT_HEADS * FOX_HEAD_DIM), lambda bi, hp, i: (bi, i, hp)),
        compiler_params=_params(("arbitrary", "arbitrary", "arbitrary")),
        name="fox_attn",
    )(qt_aug, k_aug, vt)


def _post_mix_kernel(gu_ref, vn_ref, sa_ref, sb_ref, yb_ref, xn_ref, km_ref, vm_ref,
                     wsp_ref, bsp_ref, wa_ref, wb_ref, wo_ref, g1_ref, b1_ref,
                     wmq_ref, wmo_ref, g2_ref, b2_ref, wr_ref, br_ref,
                     x2_ref, x2b_ref, ri_ref, rw_ref, cnt_ref):
    tm = gu_ref.shape[0]
    subs = [slice(s0, s0 + POST_SUB) for s0 in range(0, tm, POST_SUB)]
    each = lambda fn, *lists: [fn(*args) for args in zip(*lists)] if lists else [fn(sl) for sl in subs]
    lane = lax.broadcasted_iota(jnp.int32, (GM_BLOCK, LANES), 1)

    def spatial(sl):
        blocks = []
        for r0 in range(sl.start, sl.stop, GM_BLOCK):
            pairs = []
            for p in range(GM_GROUPS // 2):
                vp = vn_ref[r0:r0 + GM_BLOCK, p * LANES:(p + 1) * LANES]
                lo = _dot(wsp_ref[2 * p], vp)
                hi = _dot(wsp_ref[2 * p + 1], vp)
                pairs.append(jnp.where(lane < GM_WIDTH // GM_GROUPS, lo, hi))
            blocks.append(jnp.concatenate(pairs, axis=1) + bsp_ref[...])
        return (gu_ref[sl, :].astype(f32) * jnp.concatenate(blocks, axis=0)).astype(bf16)

    y_a = each(spatial)
    br_a = each(lambda y: _dot(y, wa_ref[...]), y_a)
    br_b = each(lambda sl: _dot(yb_ref[sl, :], wb_ref[...]))
    merged = each(lambda sl, a, b: (sa_ref[sl, :].astype(f32) * a + sb_ref[sl, :].astype(f32) * b).astype(bf16),
                  subs, br_a, br_b)
    proj = each(lambda m: _dot(m, wo_ref[...]), merged)
    x1 = each(lambda sl, p: _ln(DEEPNORM_ALPHA * xn_ref[sl, :] + p, g1_ref[...], b1_ref[...]), subs, proj)
    qm = each(lambda v: (_dot(v.astype(bf16), wmq_ref[...]) * (MEM_HEAD_DIM ** -0.5)).astype(bf16), x1)
    heads = [[] for _ in subs]
    for h in range(MEM_HEADS):
        sl_h = slice(h * MEM_HEAD_DIM, (h + 1) * MEM_HEAD_DIM)
        sc = each(lambda q: _dot_nt(q[:, sl_h], km_ref[:, sl_h]), qm)
        pr = each(lambda s: jnp.exp(s - jnp.max(s, axis=-1, keepdims=True)), sc)
        pr = each(lambda p: (p / jnp.sum(p, axis=-1, keepdims=True)).astype(bf16), pr)
        for hs, p in zip(heads, pr):
            hs.append(_dot(p, vm_ref[:, sl_h]))
    o = each(lambda hs: jnp.concatenate(hs, axis=1).astype(bf16), heads)
    proj2 = each(lambda v: _dot(v, wmo_ref[...]), o)
    x2 = each(lambda a, p: _ln(DEEPNORM_ALPHA * a + p, g2_ref[...], b2_ref[...]), x1, proj2)

    def route(sl, v):
        x2_ref[sl, :] = v
        x2b_ref[sl, :] = v.astype(bf16)
        v_hi = v.astype(bf16)
        v_lo = (v - v_hi.astype(f32)).astype(bf16)
        both = _dot(v_hi, wr_ref[...])
        logits = both[:, :LANES] + both[:, LANES:] + _dot(v_lo, wr_ref[:, :LANES]) + br_ref[...]
        n = v.shape[0]
        lane_t = lax.broadcasted_iota(jnp.int32, (n, LANES), 1)
        work = jnp.where(lane_t < N_EXPERTS, logits, -jnp.inf)
        tops, idxs = [], []
        for _ in range(TOP_K):
            mk = jnp.max(work, axis=-1, keepdims=True)
            ik = jnp.min(jnp.where(work == mk, lane_t, LANES), axis=-1, keepdims=True)
            work = jnp.where(lane_t == ik, -jnp.inf, work)
            tops.append(mk)
            idxs.append(ik)
        exps = [jnp.exp(tk - tops[0]) for tk in tops]
        denom = exps[0] + exps[1] + exps[2] + exps[3]
        member = jnp.zeros((n, LANES), f32)
        ri = jnp.zeros((n, LANES), jnp.int32)
        rw = jnp.zeros((n, LANES), f32)
        for k in range(TOP_K):
            member = member + jnp.where(lane_t == idxs[k], 1.0, 0.0)
            ri = jnp.where(lane_t == k, idxs[k], ri)
            rw = jnp.where(lane_t == k, exps[k] / denom, rw)
        ri_ref[sl, :] = ri
        rw_ref[sl, :] = rw
        return jnp.sum(member, axis=0, keepdims=True)

    counts = each(route, subs, x2)
    cnt_ref[0] = functools.reduce(lambda a, b: a + b, counts)


def _post_mix(gu, vn, sa, sb, yb, xn, k_mem, v_mem, wsp, bsp, wa, wb, wo, g1, b1, wmq, wmo, g2, b2, wr, br,
              seq, m_len):
    t = gu.shape[0]
    tm = TM_POST
    per_seq = seq // tm
    row = lambda w: pl.BlockSpec((tm, w), lambda i: (i, 0))
    memspec = pl.BlockSpec((m_len, D_MODEL), lambda i: (i // per_seq, 0))
    consts = [wsp, bsp, wa, wb, wo, g1, b1, wmq, wmo, g2, b2, wr, br]
    return pl.pallas_call(
        _post_mix_kernel,
        out_shape=(jax.ShapeDtypeStruct((t, D_MODEL), f32),
                   jax.ShapeDtypeStruct((t, D_MODEL), bf16),
                   jax.ShapeDtypeStruct((t, LANES), jnp.int32),
                   jax.ShapeDtypeStruct((t, LANES), f32),
                   jax.ShapeDtypeStruct((t // tm, 1, LANES), f32)),
        grid=(t // tm,),
        in_specs=[row(GM_WIDTH), row(GM_WIDTH), row(D_MODEL), row(D_MODEL), row(FOX_WIDTH), row(D_MODEL),
                  memspec, memspec] + [_const_spec(c.shape) for c in consts],
        out_specs=(row(D_MODEL), row(D_MODEL), row(LANES), row(LANES),
                   pl.BlockSpec((1, 1, LANES), lambda i: (i, 0, 0))),
        compiler_params=_params(("arbitrary",)),
        name="post_mix",
    )(gu, vn, sa, sb, yb, xn, k_mem, v_mem, *consts)


def _zero_fill_copies(n_s, dst_s, zero_ref, global_ref, sem, op):
    def per_range(r, _):
        n = n_s[r]
        gd = dst_s[r]
        for size in PIECE_SIZES:
            done = (n // (2 * size)) * (2 * size)

            @pl.when((n & size) != 0)
            def _(size=size, done=done):
                go = pl.multiple_of((gd + done) * ROW_ALIGN, ROW_ALIGN)
                cp = pltpu.make_async_copy(zero_ref.at[pl.ds(0, size * ROW_ALIGN)],
                                           global_ref.at[pl.ds(go, size * ROW_ALIGN)], sem)
                cp.start() if op == "start" else cp.wait()

        return 0

    lax.fori_loop(0, n_s.shape[0], per_range, 0)


def _piece_copies(npc_s, psrc_s, pdst_s, tile, local_ref, global_ref, sem, to_global, op):
    for c, size in enumerate(PIECE_SIZES):
        cls = tile * len(PIECE_SIZES) + c

        def one(k, _, size=size, cls=cls):
            lo = pl.multiple_of(psrc_s[cls * N_EXPERTS + k] * ROW_ALIGN, ROW_ALIGN)
            go = pl.multiple_of(pdst_s[cls * N_EXPERTS + k] * ROW_ALIGN, ROW_ALIGN)
            loc = local_ref.at[pl.ds(lo, size * ROW_ALIGN)]
            glo = global_ref.at[pl.ds(go, size * ROW_ALIGN)]
            cp = pltpu.make_async_copy(loc, glo, sem) if to_global else pltpu.make_async_copy(glo, loc, sem)
            cp.start() if op == "start" else cp.wait()
            return 0

        lax.fori_loop(0, npc_s[cls], one, 0)


def _dispatch_kernel(npc_s, psrc_s, pdst_s, zn_s, zdst_s, ri_ref, lstc_ref, x_ref, xs_ref, xg_ref, zero_ref, sem):
    i = pl.program_id(0)
    slot = i % 2
    rt = x_ref.shape[0]
    idx_t = jnp.transpose(ri_ref[...].astype(f32))
    sub = lax.broadcasted_iota(jnp.int32, (LANES, rt), 0).astype(f32)
    member_t = jnp.zeros((LANES, rt), f32)
    for k in range(TOP_K):
        member_t = member_t + jnp.where(sub == idx_t[k:k + 1, :], 1.0, 0.0)
    r_ids = lax.broadcasted_iota(jnp.int32, (rt, rt), 0)
    c_ids = lax.broadcasted_iota(jnp.int32, (rt, rt), 1)
    earlier = jnp.where(r_ids < c_ids, 1.0, 0.0).astype(bf16)
    base = lstc_ref[0] + _dot(member_t.astype(bf16), earlier)
    lpos = [jnp.sum(jnp.where(sub == idx_t[k:k + 1, :], base, 0.0), axis=0, keepdims=True).astype(jnp.int32)
            for k in range(TOP_K)]
    blk = RUN_ROWS // SEL_BLOCKS
    row_i = lax.broadcasted_iota(jnp.int32, (blk, rt), 0)
    sels = []
    for r in range(SEL_BLOCKS):
        sel = jnp.zeros((blk, rt), f32)
        for k in range(TOP_K):
            sel = jnp.where(row_i == lpos[k] - r * blk, 1.0, sel)
        sels.append(sel.astype(bf16))
    for r, sel in enumerate(sels):
        xg_ref[slot, r * blk:(r + 1) * blk, :] = _dot(sel, x_ref[...]).astype(bf16)
    _piece_copies(npc_s, psrc_s, pdst_s, i, xg_ref.at[slot], xs_ref, sem.at[slot], True, "start")

    @pl.when(i > 0)
    def _():
        _piece_copies(npc_s, psrc_s, pdst_s, i - 1, xg_ref.at[1 - slot], xs_ref, sem.at[1 - slot], True, "wait")

    @pl.when(i == pl.num_programs(0) - 1)
    def _():
        zero_ref[...] = jnp.zeros_like(zero_ref)
        _zero_fill_copies(zn_s, zdst_s, zero_ref, xs_ref, sem.at[2], "start")
        _zero_fill_copies(zn_s, zdst_s, zero_ref, xs_ref, sem.at[2], "wait")
        _piece_copies(npc_s, psrc_s, pdst_s, i, xg_ref.at[slot], xs_ref, sem.at[slot], True, "wait")


def _dispatch(npc_s, psrc_s, pdst_s, zn_s, zdst_s, ri, lst_col, x2b, n_rows):
    t = x2b.shape[0]
    grid_spec = pltpu.PrefetchScalarGridSpec(
        num_scalar_prefetch=5,
        grid=(t // RT,),
        in_specs=[pl.BlockSpec((RT, LANES), lambda i, *_: (i, 0)),
                  pl.BlockSpec((1, LANES, 1), lambda i, *_: (i, 0, 0)),
                  pl.BlockSpec((RT, D_MODEL), lambda i, *_: (i, 0))],
        out_specs=pl.BlockSpec(memory_space=pl.ANY),
        scratch_shapes=[pltpu.VMEM((2, RUN_ROWS, D_MODEL), bf16), pltpu.VMEM((RT, D_MODEL), bf16),
                        pltpu.SemaphoreType.DMA((3,))],
    )
    return pl.pallas_call(
        _dispatch_kernel,
        out_shape=jax.ShapeDtypeStruct((n_rows, D_MODEL), bf16),
        grid_spec=grid_spec,
        compiler_params=_params(("arbitrary",)),
        name="dispatch",
    )(npc_s, psrc_s, pdst_s, zn_s, zdst_s, ri, lst_col, x2b)


def _experts_kernel(et_s, newexp_s, nu_s, xs_ref, wgu_ref, bgu_ref, wd_ref, bd_ref, ys_ref, wgu_bf, wd_bf):
    w = pl.program_id(0)

    @pl.when(w < nu_s[0])
    def _():
        @pl.when(newexp_s[w] == 1)
        def _():
            wgu_bf[...] = wgu_ref[0].astype(bf16)
            wd_bf[...] = wd_ref[0].astype(bf16)

        h = _dot(xs_ref[...], wgu_bf[...]) + bgu_ref[0]
        gate = jnp.minimum(h[:, :D_EXPERT], SWIGLU_LIMIT)
        up = jnp.clip(h[:, D_EXPERT:], -SWIGLU_LIMIT, SWIGLU_LIMIT)
        glu = gate / (1.0 + jnp.exp2(gate * (-SWIGLU_ALPHA * LOG2E)))
        ys_ref[...] = (_dot(((up + 1.0) * glu).astype(bf16), wd_bf[...]) + bd_ref[0]).astype(bf16)

    @pl.when(w >= nu_s[0])
    def _():
        ys_ref[...] = jnp.zeros_like(ys_ref)


def _experts(et, newexp, n_used, xs, wgu, bgu, wd, bd):
    n_rows = xs.shape[0]
    tm = TM_EXP
    used = lambda w, nu: jnp.minimum(w, nu[0] - 1)
    wsel = lambda w, et, newexp, nu: (et[used(w, nu)], 0, 0)
    grid_spec = pltpu.PrefetchScalarGridSpec(
        num_scalar_prefetch=3,
        grid=(n_rows // tm,),
        in_specs=[pl.BlockSpec((tm, D_MODEL), lambda w, et, newexp, nu: (used(w, nu), 0)),
                  pl.BlockSpec((1, D_MODEL, 2 * D_EXPERT), wsel),
                  pl.BlockSpec((1, 1, 2 * D_EXPERT), wsel),
                  pl.BlockSpec((1, D_EXPERT, D_MODEL), wsel),
                  pl.BlockSpec((1, 1, D_MODEL), wsel)],
        out_specs=pl.BlockSpec((tm, D_MODEL), lambda w, et, newexp, nu: (w, 0)),
        scratch_shapes=[pltpu.VMEM((D_MODEL, 2 * D_EXPERT), bf16), pltpu.VMEM((D_EXPERT, D_MODEL), bf16)],
    )
    return pl.pallas_call(
        _experts_kernel,
        out_shape=jax.ShapeDtypeStruct((n_rows, D_MODEL), bf16),
        grid_spec=grid_spec,
        compiler_params=_params(("arbitrary",)),
        name="experts",
    )(et, newexp, n_used, xs, wgu, bgu, wd, bd)


def _combine_kernel(npc_s, psrc_s, pdst_s, ri_ref, rw_ref, lstr_ref, x2_ref, g_ref, b_ref, ys_ref, o_ref,
                    yg_ref, sem):
    i = pl.program_id(0)
    slot = i % 2
    rt = x2_ref.shape[0]

    @pl.when(i == 0)
    def _():
        yg_ref[...] = jnp.zeros_like(yg_ref)
        _piece_copies(npc_s, psrc_s, pdst_s, i, yg_ref.at[slot], ys_ref, sem.at[slot], False, "start")

    @pl.when(i + 1 < pl.num_programs(0))
    def _():
        _piece_copies(npc_s, psrc_s, pdst_s, i + 1, yg_ref.at[1 - slot], ys_ref, sem.at[1 - slot], False, "start")

    _piece_copies(npc_s, psrc_s, pdst_s, i, yg_ref.at[slot], ys_ref, sem.at[slot], False, "wait")

    idx = ri_ref[...]
    rw = rw_ref[...]
    lane = lax.broadcasted_iota(jnp.int32, (rt, LANES), 1)
    member = jnp.zeros((rt, LANES), f32)
    for k in range(TOP_K):
        member = member + jnp.where(lane == idx[:, k:k + 1], 1.0, 0.0)
    r_ids = lax.broadcasted_iota(jnp.int32, (rt, rt), 0)
    c_ids = lax.broadcasted_iota(jnp.int32, (rt, rt), 1)
    earlier = jnp.where(r_ids > c_ids, 1.0, 0.0).astype(bf16)
    base = lstr_ref[0] + _dot(earlier, member.astype(bf16))
    lpos = [jnp.sum(jnp.where(lane == idx[:, k:k + 1], base, 0.0), axis=-1, keepdims=True).astype(jnp.int32)
            for k in range(TOP_K)]
    blk = rt // SEL_BLOCKS
    lane_w = lax.broadcasted_iota(jnp.int32, (blk, RUN_ROWS), 1)
    sels = []
    for r in range(SEL_BLOCKS):
        rows = slice(r * blk, (r + 1) * blk)
        sel = jnp.zeros((blk, RUN_ROWS), f32)
        for k in range(TOP_K):
            sel = jnp.where(lane_w == lpos[k][rows], rw[rows, k:k + 1], sel)
        sels.append(sel.astype(bf16))
    for r, sel in enumerate(sels):
        rows = slice(r * blk, (r + 1) * blk)
        y = _dot(sel, yg_ref[slot])
        o_ref[rows, :] = _ln(DEEPNORM_ALPHA * x2_ref[rows, :] + y, g_ref[...], b_ref[...])


def _combine(npc_s, psrc_s, pdst_s, ri, rw, lst_row, x2, g3, b3, ys):
    t = x2.shape[0]
    row = lambda w: pl.BlockSpec((RT, w), lambda i, *_: (i, 0))
    const = lambda a: pl.BlockSpec(a.shape, lambda i, *_: (0,) * a.ndim)
    grid_spec = pltpu.PrefetchScalarGridSpec(
        num_scalar_prefetch=3,
        grid=(t // RT,),
        in_specs=[row(LANES), row(LANES), pl.BlockSpec((1, 1, LANES), lambda i, *_: (i, 0, 0)),
                  row(D_MODEL), const(g3), const(b3), pl.BlockSpec(memory_space=pl.ANY)],
        out_specs=row(D_MODEL),
        scratch_shapes=[pltpu.VMEM((2, RUN_ROWS, D_MODEL), bf16), pltpu.SemaphoreType.DMA((2,))],
    )
    return pl.pallas_call(
        _combine_kernel,
        out_shape=jax.ShapeDtypeStruct((t, D_MODEL), f32),
        grid_spec=grid_spec,
        compiler_params=_params(("arbitrary",)),
        name="combine",
    )(npc_s, psrc_s, pdst_s, ri, rw, lst_row, x2, g3, b3, ys)


def _augment_qk(w_qk, scale):
    d = w_qk.shape[0]
    w_h = (w_qk * scale).reshape(d, FOX_HEADS, FOX_HEAD_DIM)
    return jnp.pad(w_h, ((0, 0), (0, 0), (0, AUG_LANES - FOX_HEAD_DIM))).reshape(d, FOX_HEADS * AUG_LANES)


def _bias_lane_tables():
    src = jnp.arange(LANES)[:, None]
    dst = jnp.arange(FOX_HEADS * AUG_LANES)[None, :]
    head, off = dst // AUG_LANES, dst % AUG_LANES - FOX_HEAD_DIM
    in_q = (off >= 0) & (off < N_SPLIT)
    in_k = (off >= N_SPLIT) & (off < 2 * N_SPLIT)
    eq = jnp.where(in_q & (src == N_SPLIT * head + off), 1.0, 0.0)
    ek = jnp.where(in_k & (src == N_SPLIT * head + off - N_SPLIT), -1.0, 0.0)
    return eq.astype(bf16), ek.astype(bf16), in_k.astype(f32), in_q.astype(f32)


def _round_up(a, m):
    return (a + m - 1) // m * m


def kernel(x, mem, ln_in_g, ln_in_b, w_in, b_forget, ln_v_g, ln_v_b, w_spatial, b_spatial, w_branch_a, w_branch_b, w_out, ln1_g, ln1_b, ln_mem_g, ln_mem_b, w_mq, w_mkv, w_mo, ln2_g, ln2_b, w_router, b_router, w_gate_up, b_gate_up, w_down, b_down, ln3_g, ln3_b):
    bsz, seq, d = x.shape
    m_len = mem.shape[1]
    t = bsz * seq
    assert d == D_MODEL and w_in.shape[0] == DEPTH == 1
    assert seq % TQ == 0 and seq % TM_IN == 0 and seq % TM_POST == 0
    assert t % RT == 0 and RT % TM_POST == 0 and RT == TM_EXP
    vec = lambda a: a.reshape(1, -1).astype(f32)
    l = 0
    o = [0]
    for w in (GM_WIDTH, GM_WIDTH, FOX_WIDTH, FOX_WIDTH, FOX_WIDTH, FOX_HEADS, D_MODEL, D_MODEL):
        o.append(o[-1] + w)
    wi = w_in[l]
    w_f = wi[:, o[5]:o[6]]
    wuv = wi[:, o[0]:o[2]].astype(bf16)
    wq = _augment_qk(wi[:, o[2]:o[3]], FOX_HEAD_DIM ** -0.5 * LOG2E).astype(bf16)
    wk = _augment_qk(wi[:, o[3]:o[4]], 1.0).astype(bf16)
    wv_h = wi[:, o[4]:o[5]].T.reshape(FOX_HEADS, FOX_HEAD_DIM, d)
    wvt = jnp.pad(wv_h, ((0, 0), (0, VT_ROWS - FOX_HEAD_DIM), (0, 0))).reshape(FOX_HEADS * VT_ROWS, d).astype(bf16)
    vone = jnp.tile((jnp.arange(VT_ROWS) >= FOX_HEAD_DIM).astype(f32), FOX_HEADS).reshape(-1, 1)
    wg = wi[:, o[6]:o[8]].astype(bf16)
    rep3 = lambda a: jnp.pad(jnp.repeat(a, N_SPLIT, axis=-1), ((0, 0), (0, LANES - N_SPLIT * FOX_HEADS)))
    wf = rep3(w_f).astype(bf16)
    bf3 = rep3(b_forget[l][None, :].astype(f32))
    eq, ek, cq, ck = _bias_lane_tables()
    pos = jnp.arange(GM_BLOCK)
    chunk_mask = (pos[None, :] // CHUNK) <= (pos[:, None] // CHUNK)
    wsp = jnp.where(chunk_mask[None], w_spatial[l], 0).astype(bf16)
    bsp = jnp.repeat(b_spatial[l].T, GM_WIDTH // GM_GROUPS, axis=1).astype(f32)
    wr_f = jnp.pad(w_router[l], ((0, 0), (0, LANES - N_EXPERTS))).astype(f32)
    wr_hi = wr_f.astype(bf16)
    wr = jnp.concatenate([wr_hi, (wr_f - wr_hi.astype(f32)).astype(bf16)], axis=1)
    br = jnp.pad(b_router[l], (0, LANES - N_EXPERTS)).reshape(1, LANES).astype(f32)

    k_mem, v_mem = _mem_kv(mem.reshape(bsz * m_len, d), vec(ln_mem_g[l]), vec(ln_mem_b[l]),
                           w_mkv[l].astype(bf16), m_len)
    xn, gu, vn, qt_aug, k_aug, vt, sa, sb = _in_proj(
        x.reshape(t, d), vec(ln_in_g), vec(ln_in_b), wuv, wq, wk, wvt, vone, wg, wf, bf3, eq, ek, cq, ck,
        vec(ln_v_g[l]), vec(ln_v_b[l]), seq)
    yb = _fox_attn(qt_aug, k_aug.reshape(bsz, seq, -1), vt, bsz, seq)
    x2, x2b, ri, rw, cnt_tiles = _post_mix(
        gu, vn, sa, sb, yb.reshape(t, FOX_WIDTH), xn, k_mem, v_mem, wsp, bsp,
        w_branch_a[l].astype(bf16), w_branch_b[l].astype(bf16), w_out[l].astype(bf16),
        vec(ln1_g[l]), vec(ln1_b[l]), w_mq[l].astype(bf16), w_mo[l].astype(bf16),
        vec(ln2_g[l]), vec(ln2_b[l]), wr, br, seq, m_len)

    i32 = jnp.int32
    n_rt = t // RT
    cnt = cnt_tiles[:, 0, :N_EXPERTS].astype(i32).reshape(n_rt, RT // TM_POST, N_EXPERTS).sum(axis=1)
    cnt_pad = _round_up(cnt, ROW_ALIGN)
    lst = jnp.cumsum(cnt_pad, axis=1) - cnt_pad
    rows_e = cnt_pad.sum(axis=0)
    region = _round_up(rows_e, TM_EXP)
    e_end = jnp.cumsum(region)
    e_off = e_end - region
    dst = e_off[None, :] + jnp.cumsum(cnt_pad, axis=0) - cnt_pad
    n_tiles = -(-(t * TOP_K + n_rt * N_EXPERTS * (ROW_ALIGN - 1)) // TM_EXP) + N_EXPERTS
    n_used = (e_end[-1:] // TM_EXP).astype(i32)
    tile_ids = jnp.arange(n_tiles, dtype=i32)
    e_t = jnp.minimum(jnp.sum(tile_ids[:, None] * TM_EXP >= e_end[None, :], axis=1), N_EXPERTS - 1).astype(i32)
    newexp = jnp.concatenate([jnp.ones((1,), i32), (e_t[1:] != e_t[:-1]).astype(i32)])
    n_tail = n_tiles - t * TOP_K // TM_EXP
    tail_tile = n_used[0] + jnp.arange(n_tail, dtype=i32)
    z_dst = jnp.concatenate([e_off + rows_e, tail_tile * TM_EXP])
    z_n = jnp.concatenate([region - rows_e, jnp.where(tail_tile < n_tiles, TM_EXP, 0)])
    units = lambda a: (a.reshape(-1) // ROW_ALIGN).astype(i32)
    lst_pad = jnp.pad(lst, ((0, 0), (0, LANES - N_EXPERTS))).astype(f32)
    n_u, lst_u, dst_u = cnt_pad // ROW_ALIGN, lst // ROW_ALIGN, dst // ROW_ALIGN
    sizes = jnp.asarray(PIECE_SIZES, i32)[None, :, None]
    has = ((n_u[:, None, :] & sizes) != 0).astype(i32)
    done_u = n_u[:, None, :] // (2 * sizes) * (2 * sizes)
    slot_of = jnp.cumsum(has, axis=-1) - 1
    put = has[..., None, :] * (slot_of[..., None, :] == jnp.arange(N_EXPERTS, dtype=i32)[:, None])
    listed = lambda a: jnp.sum(put * (a[:, None, :] + done_u)[..., None, :], axis=-1).reshape(-1).astype(i32)
    npc = has.sum(axis=-1).reshape(-1).astype(i32)
    psrc, pdst = listed(lst_u), listed(dst_u)

    xs = _dispatch(npc, psrc, pdst, units(z_n), units(z_dst), ri, lst_pad[:, :, None], x2b, n_tiles * TM_EXP)
    ys = _experts(e_t, newexp, n_used, xs, w_gate_up[l], b_gate_up[l][:, None, :].astype(f32),
                  w_down[l], b_down[l][:, None, :].astype(f32))
    out = _combine(npc, psrc, pdst, ri, rw, lst_pad[:, None, :], x2, vec(ln3_g[l]), vec(ln3_b[l]), ys)
    return out.reshape(bsz, seq, d)
```
